```python
import math
import jax, jax.numpy as jnp
from jax import lax
import numpy as np

D_MODEL = 1024
BATCH = 2
SEQ = 8192
DEPTH = 2

GRID_W = 64
CTX_LEN = 256
F_GROUPS = 4
F_GROUP_DIM = 128
F_WIDTH = F_GROUPS * F_GROUP_DIM
DN_HEADS = 4
DN_HEAD_DIM = 128
DN_WIDTH = DN_HEADS * DN_HEAD_DIM
DN_CONV = 5
DN_CHUNK = 64
DA_HEADS = 4
DA_QK_DIM = 64
DA_V_DIM = 2 * DA_QK_DIM
DA_QK_WIDTH = DA_HEADS * 2 * DA_QK_DIM
DA_WIDTH = DA_HEADS * DA_V_DIM
Q_BLOCK = 128
ROPE_THETA = 10000.0
D_FF = 4 * D_MODEL
N_BRANCH = 3
NORM_EPS = 1e-6
IN_SPLITS = (F_WIDTH, 3 * DN_WIDTH, DN_WIDTH, 4 * DN_HEADS, DA_QK_WIDTH, DA_QK_WIDTH, DA_WIDTH, N_BRANCH * D_MODEL)
D_IN = sum(IN_SPLITS)

kernel_name = 'hybrid_fourier_deltanet_diffattn_dit'


def rms_norm(x, gain):
    xf = x.astype(jnp.float32)
    y = xf * lax.rsqrt(jnp.mean(xf * xf, axis=-1, keepdims=True) + NORM_EPS)
    return (y * gain.astype(jnp.float32)).astype(x.dtype)


def l2_normalize(x):
    return x * lax.rsqrt(jnp.sum(x * x, axis=-1, keepdims=True) + NORM_EPS)


def split_columns(p):
    offsets = []
    acc = 0
    for n in IN_SPLITS[:-1]:
        acc += n
        offsets.append(acc)
    return jnp.split(p, offsets, axis=-1)


def axial_rope_tables(rows):
    t = jnp.arange(rows * GRID_W)
    row = (t // GRID_W).astype(jnp.float32)
    col = (t % GRID_W).astype(jnp.float32)
    n_freq = DA_QK_DIM // 4
    inv_freq = ROPE_THETA ** (-jnp.arange(n_freq, dtype=jnp.float32) / n_freq)
    ang_r = row[:, None] * inv_freq[None, :]
    ang_c = col[:, None] * inv_freq[None, :]
    return (jnp.cos(ang_r), jnp.sin(ang_r), jnp.cos(ang_c), jnp.sin(ang_c))


def rotate_pairs(x, cos, sin):
    x1, x2 = jnp.split(x, 2, axis=-1)
    return jnp.concatenate([x1 * cos - x2 * sin, x2 * cos + x1 * sin], axis=-1)


def apply_axial_rope(x, tabs):
    cos_r, sin_r, cos_c, sin_c = (tab[:, None, None, :].astype(x.dtype) for tab in tabs)
    x_r, x_c = jnp.split(x, 2, axis=-1)
    return jnp.concatenate([rotate_pairs(x_r, cos_r, sin_r), rotate_pairs(x_c, cos_c, sin_c)], axis=-1)


def centred_depthwise_conv(u, w):
    pad = w.shape[0] // 2
    return lax.conv_general_dilated(u, w[:, None, :].astype(u.dtype), (1,), [(pad, pad)],
                                    dimension_numbers=('NWC', 'WIO', 'NWC'),
                                    feature_group_count=u.shape[-1])


def fourier_mix(u):
    b, t, _ = u.shape
    uf = u.astype(jnp.float32).reshape(b, t, F_GROUPS, F_GROUP_DIM)
    y = jnp.fft.fft2(uf, axes=(1, 3), norm='ortho').real
    return y.reshape(b, t, F_WIDTH).astype(u.dtype)


def deltanet_inputs(qkv, ab, conv_w, a_log, dt_bias):
    b, t, _ = qkv.shape
    qkv = jax.nn.silu(centred_depthwise_conv(qkv, conv_w)).astype(jnp.float32)
    q, k, v = jnp.split(qkv, 3, axis=-1)
    q = l2_normalize(q.reshape(b, t, DN_HEADS, DN_HEAD_DIM)) * (DN_HEAD_DIM ** -0.5)
    k = l2_normalize(k.reshape(b, t, DN_HEADS, DN_HEAD_DIM))
    v = v.reshape(b, t, DN_HEADS, DN_HEAD_DIM)
    ab = ab.astype(jnp.float32).reshape(b, t, 4, DN_HEADS)
    beta = jax.nn.sigmoid(ab[:, :, 0:2])
    g = -jnp.exp(a_log.astype(jnp.float32)) * jax.nn.softplus(ab[:, :, 2:4] + dt_bias.astype(jnp.float32))
    return (q, k, v, g, beta)


def gated_delta_chunked(q, k, v, g, beta, state0, with_out):
    b, t, h, _ = k.shape
    dv = v.shape[-1]
    n = t // DN_CHUNK

    def to_chunks(z):
        return z.reshape(b, n, DN_CHUNK, h, -1).transpose(1, 0, 3, 2, 4)

    kc, vc = to_chunks(k), to_chunks(v)
    bc = to_chunks(beta[..., None])
    gcum = jnp.cumsum(to_chunks(g[..., None])[..., 0], axis=-1)
    idx = jnp.arange(DN_CHUNK)
    lower = idx[:, None] >= idx[None, :]
    strict = idx[:, None] > idx[None, :]
    decay = jnp.exp(jnp.where(lower, gcum[..., :, None] - gcum[..., None, :], -jnp.inf))
    kb = kc * bc
    lmat = jnp.where(strict, jnp.einsum('nbhid,nbhjd->nbhij', kb, kc) * decay, 0.0)
    eye = jnp.broadcast_to(jnp.eye(DN_CHUNK, dtype=jnp.float32), lmat.shape)
    tmat = lax.linalg.triangular_solve(eye + lmat, eye, left_side=True, lower=True, unit_diagonal=True)
    u = tmat @ (vc * bc)
    w = tmat @ (kb * jnp.exp(gcum)[..., None])
    g_last = gcum[..., -1]
    k_to_end = kc * jnp.exp(g_last[..., None] - gcum)[..., None]

    if with_out:
        qc = to_chunks(q)
        qk = jnp.einsum('nbhid,nbhjd->nbhij', qc, kc) * decay
        q_dec = qc * jnp.exp(gcum)[..., None]

        def step(s, xs):
            u_i, w_i, ke_i, gl_i, qd_i, qk_i = xs
            v_new = u_i - w_i @ s
            out = qd_i @ s + qk_i @ v_new
            s = s * jnp.exp(gl_i)[..., None, None] + jnp.einsum('bhcd,bhce->bhde', ke_i, v_new)
            return s, out

        s, outs = lax.scan(step, state0, (u, w, k_to_end, g_last, q_dec, qk))
        return s, outs.transpose(1, 0, 3, 2, 4).reshape(b, t, h, dv)

    def step_state(s, xs):
        u_i, w_i, ke_i, gl_i = xs
        v_new = u_i - w_i @ s
        s = s * jnp.exp(gl_i)[..., None, None] + jnp.einsum('bhcd,bhce->bhde', ke_i, v_new)
        return s, None

    s, _ = lax.scan(step_state, state0, (u, w, k_to_end, g_last))
    return s, None


def flip_time(z):
    return jnp.flip(z, axis=1)


def bidirectional_deltanet(lat, ctx, need_ctx):
    q, k, v, g, beta = lat
    qc, kc, vc, gc, bc = ctx
    s0 = jnp.zeros((q.shape[0], DN_HEADS, DN_HEAD_DIM, DN_HEAD_DIM), jnp.float32)
    s_ctx_f, o_ctx_f = gated_delta_chunked(qc, kc, vc, gc[:, :, 0], bc[:, :, 0], s0, need_ctx)
    _, o_lat_f = gated_delta_chunked(q, k, v, g[:, :, 0], beta[:, :, 0], s_ctx_f, True)
    s_ctx_b, o_ctx_b = gated_delta_chunked(flip_time(qc), flip_time(kc), flip_time(vc), flip_time(gc[:, :, 1]),
                                           flip_time(bc[:, :, 1]), s0, need_ctx)
    _, o_lat_b = gated_delta_chunked(flip_time(q), flip_time(k), flip_time(v), flip_time(g[:, :, 1]),
                                     flip_time(beta[:, :, 1]), s_ctx_b, True)
    o_lat = o_lat_f + flip_time(o_lat_b)
    o_ctx = (o_ctx_f + flip_time(o_ctx_b)) if need_ctx else None
    return o_lat, o_ctx


def deltanet_output(o, z, gain, dtype):
    b, t = o.shape[:2]
    z = z.astype(jnp.float32).reshape(b, t, DN_HEADS, DN_HEAD_DIM)
    return (rms_norm(o, gain) * jax.nn.silu(z)).astype(dtype).reshape(b, t, DN_WIDTH)


def diff_attend(q, k, v, lam):
    s = jnp.einsum('bqhmd,bkhmd->bhmqk', q, k, preferred_element_type=jnp.float32) * (DA_QK_DIM ** -0.5)
    p = jax.nn.softmax(s, axis=-1)
    a = p[:, :, 0] - lam * p[:, :, 1]
    return jnp.einsum('bhqk,bkhe->bqhe', a.astype(v.dtype), v)


def differential_attention_latent(q, k_lat, v_lat, k_ctx, v_ctx, lam):
    b, t = q.shape[:2]
    k_all = jnp.concatenate([k_ctx, k_lat], axis=1)
    v_all = jnp.concatenate([v_ctx, v_lat], axis=1)
    n_blk = t // Q_BLOCK
    q_blocks = jnp.swapaxes(q.reshape(b, n_blk, Q_BLOCK, DA_HEADS, 2, DA_QK_DIM), 0, 1)
    o = lax.map(lambda qb: diff_attend(qb, k_all, v_all, lam), q_blocks)
    return jnp.swapaxes(o, 0, 1).reshape(b, t, DA_HEADS, DA_V_DIM)


def diff_output(o, gain, lam_init):
    b, t = o.shape[:2]
    return (rms_norm(o, gain) * (1.0 - lam_init)).reshape(b, t, DA_WIDTH)


def gated_merge(y_f, y_dn, y_da, gates):
    g_f, g_dn, g_da = jnp.split(jax.nn.sigmoid(gates.astype(jnp.float32)).astype(gates.dtype), N_BRANCH, axis=-1)
    return g_f * y_f + g_dn * y_dn + g_da * y_da


def hybrid_mixer(h, hc, w_in, conv_w, a_log, dt_bias, dn_gain, lam_vecs, da_gain, w_f, w_dn, w_da, w_o,
                 tabs, lam_init, need_ctx):
    b, t, _ = h.shape
    tc = hc.shape[1]
    u_f, dn_qkv, dn_z, dn_ab, da_q, da_k, da_v, gates = split_columns(h @ w_in)
    u_fc, dn_qkvc, dn_zc, dn_abc, da_qc, da_kc, da_vc, gatesc = split_columns(hc @ w_in)

    y_f = fourier_mix(u_f) @ w_f
    o_dn, o_dnc = bidirectional_deltanet(deltanet_inputs(dn_qkv, dn_ab, conv_w, a_log, dt_bias),
                                         deltanet_inputs(dn_qkvc, dn_abc, conv_w, a_log, dt_bias), need_ctx)
    y_dn = deltanet_output(o_dn, dn_z, dn_gain, h.dtype) @ w_dn
    lq1, lk1, lq2, lk2 = lam_vecs.astype(jnp.float32)
    lam = jnp.exp(jnp.sum(lq1 * lk1)) - jnp.exp(jnp.sum(lq2 * lk2)) + lam_init
    q_l = apply_axial_rope(da_q.reshape(b, t, DA_HEADS, 2, DA_QK_DIM), tabs)
    k_l = apply_axial_rope(da_k.reshape(b, t, DA_HEADS, 2, DA_QK_DIM), tabs)
    v_l = da_v.reshape(b, t, DA_HEADS, DA_V_DIM)
    k_c = da_kc.reshape(b, tc, DA_HEADS, 2, DA_QK_DIM)
    v_c = da_vc.reshape(b, tc, DA_HEADS, DA_V_DIM)
    o_da = differential_attention_latent(q_l, k_l, v_l, k_c, v_c, lam)
    y_da = diff_output(o_da, da_gain, lam_init) @ w_da
    y = gated_merge(y_f, y_dn, y_da, gates) @ w_o
    if not need_ctx:
        return y, None
    y_fc = fourier_mix(u_fc) @ w_f
    y_dnc = deltanet_output(o_dnc, dn_zc, dn_gain, hc.dtype) @ w_dn
    o_dac = diff_attend(da_qc.reshape(b, tc, DA_HEADS, 2, DA_QK_DIM), k_c, v_c, lam)
    y_dac = diff_output(o_dac, da_gain, lam_init) @ w_da
    yc = gated_merge(y_fc, y_dnc, y_dac, gatesc) @ w_o
    return y, yc


def squared_relu_mlp(h, w1, w2):
    a = jax.nn.relu(h @ w1)
    return (a * a) @ w2


def setup_inputs(seed: int = 0) -> dict:
    key = jax.random.key(seed)
    ks = jax.random.split(key, 24)
    f32 = jnp.float32

    def dense(k, shape, fan_in, gain=1.0):
        return jax.random.normal(k, shape, f32) * (gain * fan_in ** -0.5)

    def gain_vec(k, shape):
        return 1.0 + 0.02 * jax.random.normal(k, shape, f32)

    dt = jnp.exp(jax.random.uniform(ks[11], (DEPTH, 2, DN_HEADS), f32, math.log(1e-3), math.log(1e-1)))
    return {
        'x': jax.random.normal(ks[0], (BATCH, SEQ, D_MODEL), f32),
        'c': jax.random.normal(ks[1], (BATCH, D_MODEL), f32),
        'ctx': jax.random.normal(ks[2], (BATCH, CTX_LEN, D_MODEL), f32),
        'c_ctx': jax.random.normal(ks[3], (D_MODEL,), f32),
        'norm1': gain_vec(ks[4], (DEPTH, D_MODEL)),
        'norm2': gain_vec(ks[5], (DEPTH, D_MODEL)),
        'w_ada': dense(ks[6], (DEPTH, D_MODEL, 6 * D_MODEL), D_MODEL, 0.5),
        'b_ada': 0.02 * jax.random.normal(ks[7], (DEPTH, 6 * D_MODEL), f32),
        'w_in': dense(ks[8], (DEPTH, D_MODEL, D_IN), D_MODEL),
        'conv_w': dense(ks[9], (DEPTH, DN_CONV, 3 * DN_WIDTH), DN_CONV),
        'a_log': jnp.log(jax.random.uniform(ks[10], (DEPTH, 2, DN_HEADS), f32, 1.0, 16.0)),
        'dt_bias': dt + jnp.log(-jnp.expm1(-dt)),
        'dn_gain': gain_vec(ks[12], (DEPTH, DN_HEAD_DIM)),
        'lam_vecs': 0.1 * jax.random.normal(ks[13], (DEPTH, 4, DA_QK_DIM), f32),
        'da_gain': gain_vec(ks[14], (DEPTH, DA_V_DIM)),
        'w_f': dense(ks[15], (DEPTH, F_WIDTH, D_MODEL), F_WIDTH),
        'w_dn': dense(ks[16], (DEPTH, DN_WIDTH, D_MODEL), DN_WIDTH),
        'w_da': dense(ks[17], (DEPTH, DA_WIDTH, D_MODEL), DA_WIDTH),
        'w_o': dense(ks[18], (DEPTH, D_MODEL, D_MODEL), D_MODEL),
        'w_mlp1': dense(ks[19], (DEPTH, D_MODEL, D_FF), D_MODEL),
        'w_mlp2': dense(ks[20], (DEPTH, D_FF, D_MODEL), D_FF),
        'final_norm': gain_vec(ks[21], (D_MODEL,)),
    }


def reference(x, c, ctx, c_ctx, norm1, norm2, w_ada, b_ada, w_in, conv_w, a_log, dt_bias, dn_gain, lam_vecs,
              da_gain, w_f, w_dn, w_da, w_o, w_mlp1, w_mlp2, final_norm):
    ROWS = x.shape[1] // GRID_W
    tabs = axial_rope_tables(ROWS)
    c_act = jax.nn.silu(c)
    cc_act = jax.nn.silu(c_ctx)
    xc = ctx
    for l in range(DEPTH):
        need_ctx = l < DEPTH - 1
        lam_init = 0.8 - 0.6 * math.exp(-0.3 * l)
        sh1, sc1, g1, sh2, sc2, g2 = (m[:, None, :] for m in jnp.split(c_act @ w_ada[l] + b_ada[l], 6, axis=-1))
        sh1c, sc1c, g1c, sh2c, sc2c, g2c = jnp.split(cc_act @ w_ada[l] + b_ada[l], 6, axis=-1)
        h = rms_norm(x, norm1[l]) * (1 + sc1) + sh1
        hc = rms_norm(xc, norm1[l]) * (1 + sc1c) + sh1c
        y, yc = hybrid_mixer(h, hc, w_in[l], conv_w[l], a_log[l], dt_bias[l], dn_gain[l], lam_vecs[l], da_gain[l],
                             w_f[l], w_dn[l], w_da[l], w_o[l], tabs, lam_init, need_ctx)
        x = x + g1 * y
        h2 = rms_norm(x, norm2[l]) * (1 + sc2) + sh2
        x = x + g2 * squared_relu_mlp(h2, w_mlp1[l], w_mlp2[l])
        if need_ctx:
            xc = xc + g1c * yc
            h2c = rms_norm(xc, norm2[l]) * (1 + sc2c) + sh2c
            xc = xc + g2c * squared_relu_mlp(h2c, w_mlp1[l], w_mlp2[l])
    return rms_norm(x, final_norm)
```

```python
import functools
import math

import numpy as np
import jax
import jax.numpy as jnp
from jax import lax
from jax.experimental import pallas as pl
from jax.experimental.pallas import tpu as pltpu

F32 = jnp.float32
BF16 = jnp.bfloat16

D_MODEL = 1024
GRID_W = 64
F_GROUPS = 4
F_GROUP_DIM = 128
F_WIDTH = F_GROUPS * F_GROUP_DIM
DN_HEADS = 4
DN_HEAD_DIM = 128
DN_WIDTH = DN_HEADS * DN_HEAD_DIM
DN_CONV = 5
DN_CHUNK = 64
DA_HEADS = 4
DA_QK_DIM = 64
DA_V_DIM = 2 * DA_QK_DIM
DA_WIDTH = DA_HEADS * DA_V_DIM
ROPE_THETA = 10000.0
D_FF = 4 * D_MODEL
NORM_EPS = 1e-6

ROW_TILE = 256
CONV_HALO = 16
VMEM_LIMIT = 56 * 1024 * 1024

C_UF = 0
C_QKV = C_UF + F_WIDTH
C_Z = C_QKV + 3 * DN_WIDTH
C_Q = C_Z + DN_WIDTH
C_QP = C_Q + DA_WIDTH
C_K = C_QP + DA_WIDTH
C_KP = C_K + DA_WIDTH
C_V = C_KP + DA_WIDTH
C_GATE = C_V + DA_WIDTH
C_AB = C_GATE + 3 * D_MODEL
W_IN_COLS = C_AB + 256


def _dot(a, b):
    return jnp.dot(a, b, preferred_element_type=F32)


def _dot_f32(a, b):
    return jnp.dot(a, b, preferred_element_type=F32, precision=lax.Precision.HIGHEST)


def _dot_nt(a, b):
    return lax.dot_general(a, b, (((1,), (1,)), ((), ())), preferred_element_type=F32)


def _dot_tn(a, b):
    return lax.dot_general(a, b, (((0,), (0,)), ((), ())), preferred_element_type=F32)


def _silu(x):
    return x * jax.nn.sigmoid(x)


def _softplus(x):
    return jnp.maximum(x, 0.0) + jnp.log1p(jnp.exp(-jnp.abs(x)))


def _params(*sem):
    return pltpu.CompilerParams(dimension_semantics=sem, vmem_limit_bytes=VMEM_LIMIT)


def _resident(shape):
    nd = len(shape)
    return pl.BlockSpec(shape, lambda *_: (0,) * nd, pipeline_mode=pl.Buffered(1))


def _ada_kernel(c_ref, w_ref, b_ref, o_ref):
    a = _silu(c_ref[...])
    o_ref[0] = _dot_f32(a, w_ref[0]) + b_ref[0]


def ada_modulation(cvec, w_ada, b_ada):
    depth, d, n = w_ada.shape
    tn = 1536
    return pl.pallas_call(
        _ada_kernel,
        grid=(depth, n // tn),
        in_specs=[pl.BlockSpec((8, d), lambda l, j: (0, 0)),
                  pl.BlockSpec((1, d, tn), lambda l, j: (l, 0, j)),
                  pl.BlockSpec((1, 1, tn), lambda l, j: (l, 0, j))],
        out_specs=pl.BlockSpec((1, 8, tn), lambda l, j: (l, 0, j)),
        out_shape=jax.ShapeDtypeStruct((depth, 8, n), F32),
        compiler_params=_params("arbitrary", "arbitrary"),
        name="ada_modulation",
    )(cvec, w_ada, b_ada.reshape(depth, 1, n))


def _mod_row(mod_ref, n_lat_tiles, batch):
    i = pl.program_id(1)
    b = pl.program_id(0)
    r = jnp.where(i >= n_lat_tiles, batch, b)
    return mod_ref[pl.ds(r, 1), :]


def _rms(x):
    return x * lax.rsqrt(jnp.mean(x * x, axis=-1, keepdims=True) + NORM_EPS)


def _inproj_kernel(n_lat_tiles, batch, x_ref, mod_ref, gain_ref, w_ref, cos_ref, sin_ref, abp_ref,
                   uf_ref, qkv_ref, z_ref, q_ref, k_ref, v_ref, gate_ref, gb_ref):
    mod = _mod_row(mod_ref, n_lat_tiles, batch)
    sh = mod[:, 0:D_MODEL]
    sc = mod[:, D_MODEL:2 * D_MODEL]
    h = (_rms(x_ref[0]) * gain_ref[...]) * (1.0 + sc) + sh
    hb = h.astype(BF16)

    def proj(c0, width):
        return _dot(hb, w_ref[:, c0:c0 + width])

    uf_ref[0] = proj(C_UF, F_WIDTH).astype(BF16)
    for j in range(3):
        qkv_ref[0, :, j * DN_WIDTH:(j + 1) * DN_WIDTH] = proj(C_QKV + j * DN_WIDTH, DN_WIDTH).astype(BF16)
    z_ref[0] = proj(C_Z, DN_WIDTH).astype(BF16)
    cos = jnp.tile(cos_ref[...], (1, DA_HEADS))
    sin = jnp.tile(sin_ref[...], (1, DA_HEADS))
    q = proj(C_Q, DA_WIDTH) * cos + proj(C_QP, DA_WIDTH) * sin
    q_ref[0] = (q * (DA_QK_DIM ** -0.5)).astype(BF16)
    k = proj(C_K, DA_WIDTH) * cos + proj(C_KP, DA_WIDTH) * sin
    k_ref[0] = k.astype(BF16)
    v_ref[0] = proj(C_V, DA_WIDTH).astype(BF16)
    for j in range(3 * D_MODEL // 512):
        g = proj(C_GATE + j * 512, 512)
        gate_ref[0, :, j * 512:(j + 1) * 512] = jax.nn.sigmoid(g).astype(BF16)
    ab = proj(C_AB, 256)
    abp = abp_ref[...]
    lane = lax.broadcasted_iota(jnp.int32, ab.shape, 1) % 128
    beta = jax.nn.sigmoid(ab)
    g = -jnp.exp(abp[0:1, :]) * _softplus(ab + abp[1:2, :])
    gb_ref[0] = jnp.where(lane < DN_HEADS, beta, g)


def input_projection(x, mod, gain, w_big, cos_t, sin_t, abp, n_lat_tiles):
    batch, t_all, d = x.shape
    n_tiles = t_all // ROW_TILE
    tm = ROW_TILE

    def rows(width, dtype):
        return (pl.BlockSpec((1, tm, width), lambda b, i: (b, i, 0)),
                jax.ShapeDtypeStruct((batch, t_all, width), dtype))

    outs = [rows(F_WIDTH, BF16), rows(3 * DN_WIDTH, BF16), rows(DN_WIDTH, BF16), rows(DA_WIDTH, BF16),
            rows(DA_WIDTH, BF16), rows(DA_WIDTH, BF16), rows(3 * D_MODEL, BF16), rows(256, F32)]
    return pl.pallas_call(
        functools.partial(_inproj_kernel, n_lat_tiles, batch),
        grid=(batch, n_tiles),
        in_specs=[pl.BlockSpec((1, tm, d), lambda b, i: (b, i, 0)),
                  _resident(mod.shape), _resident(gain.shape), _resident(w_big.shape),
                  pl.BlockSpec((tm, 128), lambda b, i: (i, 0)),
                  pl.BlockSpec((tm, 128), lambda b, i: (i, 0)),
                  _resident(abp.shape)],
        out_specs=[o[0] for o in outs],
        out_shape=[o[1] for o in outs],
        compiler_params=_params("parallel", "parallel"),
        name="input_projection",
    )(x, mod, gain, w_big, cos_t, sin_t, abp)


def _dft_tables(n_pos):
    n1 = n_pos // GRID_W
    n2 = GRID_W
    a = np.arange(n1)
    ang1 = 2.0 * np.pi * ((a[:, None] * a[None, :]) % n1) / n1
    stage1 = np.concatenate([np.cos(ang1), -np.sin(ang1)], axis=0)
    k1 = np.arange(n1)[:, None, None]
    k2 = np.arange(n2)[None, :, None]
    m = np.arange(n2)[None, None, :]
    ang2 = 2.0 * np.pi * ((m * (k1 + n1 * k2)) % n_pos) / n_pos
    c2, s2 = np.cos(ang2), np.sin(ang2)
    stage2 = np.concatenate([np.concatenate([c2, s2], axis=2),
                             np.concatenate([-s2, c2], axis=2)], axis=1)
    return stage1, stage2


def _channel_table(n_pos):
    c = np.arange(F_GROUP_DIM)
    ang = 2.0 * np.pi * ((c[:, None] * c[None, :]) % F_GROUP_DIM) / F_GROUP_DIM
    scale = 1.0 / math.sqrt(n_pos * F_GROUP_DIM)
    return np.concatenate([np.cos(ang), np.sin(ang)], axis=0) * scale


def _channel_mix(pr, pi, chan):
    outs = []
    for g in range(F_GROUPS):
        sl = slice(g * F_GROUP_DIM, (g + 1) * F_GROUP_DIM)
        pg = jnp.concatenate([pr[:, sl], pi[:, sl]], axis=1).astype(BF16)
        outs.append(_dot(pg, chan))
    return jnp.concatenate(outs, axis=1)


def _fft1_kernel(n1, f_ref, x_ref, ar_ref, ai_ref):
    a = _dot(f_ref[...].astype(BF16), x_ref[0])
    ar_ref[0] = a[:n1].astype(BF16)
    ai_ref[0] = a[n1:].astype(BF16)


def _fft2_kernel(kb, m_ref, ar_ref, ai_ref, chan_ref, o_ref):
    chan = chan_ref[...].astype(BF16)
    for j in range(kb):
        a = jnp.concatenate([ar_ref[0, j], ai_ref[0, j]], axis=0)
        p = _dot(m_ref[j].astype(BF16), a)
        y = _channel_mix(p[:GRID_W], p[GRID_W:], chan)
        o_ref[0, :, j * F_WIDTH:(j + 1) * F_WIDTH] = y.astype(BF16)


def fourier_latent(uf_all, n_pos):
    batch, t_all, _ = uf_all.shape
    n1 = n_pos // GRID_W
    stage1, stage2 = _dft_tables(n_pos)
    f1 = jnp.asarray(stage1, F32)
    m2 = jnp.asarray(stage2, F32)
    chan = jnp.asarray(_channel_table(n_pos), F32)
    wide = GRID_W * F_WIDTH
    x2 = uf_all.reshape(batch, t_all // GRID_W, wide)
    tn = 4096
    ar, ai = pl.pallas_call(
        functools.partial(_fft1_kernel, n1),
        grid=(batch, wide // tn),
        in_specs=[_resident(f1.shape),
                  pl.BlockSpec((1, n1, tn), lambda b, j: (b, 0, j))],
        out_specs=[pl.BlockSpec((1, n1, tn), lambda b, j: (b, 0, j))] * 2,
        out_shape=[jax.ShapeDtypeStruct((batch, n1, wide), BF16)] * 2,
        compiler_params=_params("parallel", "parallel"),
        name="fourier_stage1",
    )(f1, x2)
    ar = ar.reshape(batch, n1, GRID_W, F_WIDTH)
    ai = ai.reshape(batch, n1, GRID_W, F_WIDTH)
    kb = 8
    y = pl.pallas_call(
        functools.partial(_fft2_kernel, kb),
        grid=(batch, n1 // kb),
        in_specs=[pl.BlockSpec((kb, 2 * GRID_W, 2 * GRID_W), lambda b, i: (i, 0, 0)),
                  pl.BlockSpec((1, kb, GRID_W, F_WIDTH), lambda b, i: (b, i, 0, 0)),
                  pl.BlockSpec((1, kb, GRID_W, F_WIDTH), lambda b, i: (b, i, 0, 0)),
                  _resident(chan.shape)],
        out_specs=pl.BlockSpec((1, GRID_W, kb * F_WIDTH), lambda b, i: (b, 0, i)),
        out_shape=jax.ShapeDtypeStruct((batch, GRID_W, n1 * F_WIDTH), BF16),
        compiler_params=_params("parallel", "parallel"),
        name="fourier_stage2",
    )(m2, ar, ai, chan)
    return y.reshape(batch, n_pos, F_WIDTH)


def _fft_ctx_kernel(tc, f_ref, x_ref, chan_ref, o_ref):
    p = _dot(f_ref[...].astype(BF16), x_ref[0])
    o_ref[0] = _channel_mix(p[:tc], p[tc:], chan_ref[...].astype(BF16)).astype(BF16)


def fourier_context(uf_all, n_pos, tc):
    batch = uf_all.shape[0]
    a = np.arange(tc)
    ang = 2.0 * np.pi * ((a[:, None] * a[None, :]) % tc) / tc
    f = jnp.asarray(np.concatenate([np.cos(ang), -np.sin(ang)], axis=0), F32)
    chan = jnp.asarray(_channel_table(tc), F32)
    return pl.pallas_call(
        functools.partial(_fft_ctx_kernel, tc),
        grid=(batch,),
        in_specs=[_resident(f.shape),
                  pl.BlockSpec((1, tc, F_WIDTH), lambda b: (b, n_pos // tc, 0)),
                  _resident(chan.shape)],
        out_specs=pl.BlockSpec((1, tc, F_WIDTH), lambda b: (b, 0, 0)),
        out_shape=jax.ShapeDtypeStruct((batch, tc, F_WIDTH), BF16),
        compiler_params=_params("parallel"),
        name="fourier_context",
    )(f, uf_all, chan)


def _dnprep_kernel(n_lat_tiles, n_tiles, cur_ref, prev_ref, next_ref, w_ref, q_ref, k_ref, v_ref, ext_ref):
    i = pl.program_id(1)
    tm = ROW_TILE
    h = CONV_HALO
    prev_ok = jnp.logical_and(i != 0, i != n_lat_tiles)
    next_ok = jnp.logical_and(i != n_lat_tiles - 1, i != n_tiles - 1)
    ext_ref[0:h, :] = jnp.where(prev_ok, prev_ref[0].astype(F32), 0.0)
    ext_ref[h:h + tm, :] = cur_ref[0].astype(F32)
    ext_ref[h + tm:h + tm + h, :] = jnp.where(next_ok, next_ref[0].astype(F32), 0.0)
    pad = DN_CONV // 2
    acc = None
    for j in range(DN_CONV):
        term = ext_ref[h - pad + j:h - pad + j + tm, :] * w_ref[j:j + 1, :]
        acc = term if acc is None else acc + term
    u = _silu(acc)

    def l2n(block):
        return block * lax.rsqrt(jnp.sum(block * block, axis=-1, keepdims=True) + NORM_EPS)

    for hd in range(DN_HEADS):
        sl = slice(hd * DN_HEAD_DIM, (hd + 1) * DN_HEAD_DIM)
        q = u[:, hd * DN_HEAD_DIM:(hd + 1) * DN_HEAD_DIM]
        k = u[:, DN_WIDTH + hd * DN_HEAD_DIM:DN_WIDTH + (hd + 1) * DN_HEAD_DIM]
        q_ref[0, :, sl] = (l2n(q) * (DN_HEAD_DIM ** -0.5)).astype(BF16)
        k_ref[0, :, sl] = l2n(k).astype(BF16)
    v_ref[0] = u[:, 2 * DN_WIDTH:].astype(BF16)


def deltanet_prep(qkv, conv_w8, n_lat_tiles):
    batch, t_all, width = qkv.shape
    tm = ROW_TILE
    n_tiles = t_all // tm
    per = tm // CONV_HALO
    last = t_all // CONV_HALO - 1
    out = (pl.BlockSpec((1, tm, DN_WIDTH), lambda b, i: (b, i, 0)),
           jax.ShapeDtypeStruct((batch, t_all, DN_WIDTH), BF16))
    return pl.pallas_call(
        functools.partial(_dnprep_kernel, n_lat_tiles, n_tiles),
        grid=(batch, n_tiles),
        in_specs=[pl.BlockSpec((1, tm, width), lambda b, i: (b, i, 0)),
                  pl.BlockSpec((1, CONV_HALO, width), lambda b, i: (b, jnp.maximum(i * per - 1, 0), 0)),
                  pl.BlockSpec((1, CONV_HALO, width), lambda b, i: (b, jnp.minimum((i + 1) * per, last), 0)),
                  _resident(conv_w8.shape)],
        out_specs=[out[0]] * 3,
        out_shape=[out[1]] * 3,
        scratch_shapes=[pltpu.VMEM((tm + 2 * CONV_HALO, width), F32)],
        compiler_params=_params("parallel", "parallel"),
        name="deltanet_prep",
    )(qkv, qkv, qkv, conv_w8)


def _dnscan_kernel(q_ref, k_ref, v_ref, gb_ref, o_ref, s_ref):
    d = pl.program_id(1)
    step = pl.program_id(2)
    c = DN_CHUNK

    @pl.when(step == 0)
    def _():
        s_ref[...] = jnp.zeros_like(s_ref)

    ii = lax.broadcasted_iota(jnp.int32, (c, c), 0)
    jj = lax.broadcasted_iota(jnp.int32, (c, c), 1)
    sgn = 1 - 2 * d
    diff = (ii - jj) * sgn
    incl = diff >= 0
    strict = diff > 0
    eye = (ii == jj).astype(F32)

    gb = gb_ref[0]
    tri = incl.astype(F32)
    gcum = _dot_f32(tri, gb)
    gcum_row = lax.dot_general(gb.T, tri, (((1,), (1,)), ((), ())), preferred_element_type=F32,
                               precision=lax.Precision.HIGHEST)
    gtot = jnp.sum(gb, axis=0, keepdims=True)

    for hd in range(DN_HEADS):
        sl = slice(hd * DN_HEAD_DIM, (hd + 1) * DN_HEAD_DIM)
        q = q_ref[0, :, sl].astype(F32)
        k = k_ref[0, :, sl].astype(F32)
        v = v_ref[0, :, sl].astype(F32)
        beta = gb[:, hd:hd + 1]
        gc = gcum[:, DN_HEADS + hd:DN_HEADS + hd + 1]
        gr = gcum_row[DN_HEADS + hd:DN_HEADS + hd + 1, :]
        gl = gtot[:, DN_HEADS + hd:DN_HEADS + hd + 1]
        decay = jnp.exp(jnp.where(incl, gc - gr, -jnp.inf))
        kb = k * beta
        k16 = k.astype(BF16)
        lmat = jnp.where(strict, _dot_nt(kb.astype(BF16), k16) * decay, 0.0)
        blk = lambda s: (ii // s) == (jj // s)
        t = eye - jnp.where(blk(2), lmat, 0.0)
        s = 2
        while s < c:
            ls = jnp.where(jnp.logical_and(blk(2 * s), jnp.logical_not(blk(s))), lmat, 0.0)
            t16 = t.astype(BF16)
            t = t - _dot(_dot(t16, ls.astype(BF16)).astype(BF16), t16)
            s *= 2
        eg = jnp.exp(gc)
        rhs = jnp.concatenate([v * beta, kb * eg], axis=1).astype(BF16)
        uw = _dot(t.astype(BF16), rhs)
        u = uw[:, :DN_HEAD_DIM]
        w = uw[:, DN_HEAD_DIM:]
        state = s_ref[hd]
        s16 = state.astype(BF16)
        qd = q * eg
        ws = _dot(jnp.concatenate([w, qd], axis=0).astype(BF16), s16)
        v_new = u - ws[:c]
        qk = _dot_nt(q.astype(BF16), k16) * decay
        v16 = v_new.astype(BF16)
        o_ref[0, 0, :, sl] = ws[c:] + _dot(qk.astype(BF16), v16)
        ke = k * jnp.exp(gl - gc)
        s_ref[hd] = state * jnp.exp(gl) + _dot_tn(ke.astype(BF16), v16)


def deltanet_scan(qn, kn, vn, gb, n_lat):
    batch, t_all, _ = qn.shape
    c = DN_CHUNK
    n_chunks = t_all // c
    n_lat_chunks = n_lat // c
    n_ctx_chunks = n_chunks - n_lat_chunks

    def chunk(d, s):
        fwd = jnp.where(s < n_ctx_chunks, n_lat_chunks + s, s - n_ctx_chunks)
        return jnp.where(d == 0, fwd, n_chunks - 1 - s)

    tok = pl.BlockSpec((1, c, DN_WIDTH), lambda b, d, s: (b, chunk(d, s), 0))
    return pl.pallas_call(
        _dnscan_kernel,
        grid=(batch, 2, n_chunks),
        in_specs=[tok, tok, tok,
                  pl.BlockSpec((1, c, 128), lambda b, d, s: (b, chunk(d, s), d))],
        out_specs=pl.BlockSpec((1, 1, c, DN_WIDTH), lambda b, d, s: (d, b, chunk(d, s), 0)),
        out_shape=jax.ShapeDtypeStruct((2, batch, t_all, DN_WIDTH), F32),
        scratch_shapes=[pltpu.VMEM((DN_HEADS, DN_HEAD_DIM, DN_HEAD_DIM), F32)],
        compiler_params=_params("parallel", "parallel", "arbitrary"),
        name="deltanet_scan",
    )(qn, kn, vn, gb)


def _attn_kernel(tq, lam_init, q_ref, k_ref, v_ref, lv_ref, gain_ref, o_ref, qs_ref, m_ref, l_ref, acc_ref):
    ki = pl.program_id(3)

    @pl.when(ki == 0)
    def _():
        q = q_ref[0]
        lane = lax.broadcasted_iota(jnp.int32, q.shape, 1)
        zero = jnp.zeros_like(q)
        qs_ref[0:tq, :] = jnp.where(lane < DA_QK_DIM, q, zero)
        qs_ref[tq:2 * tq, :] = jnp.where(lane >= DA_QK_DIM, q, zero)
        m_ref[...] = jnp.full_like(m_ref, -jnp.inf)
        l_ref[...] = jnp.zeros_like(l_ref)
        acc_ref[...] = jnp.zeros_like(acc_ref)

    s = _dot_nt(qs_ref[...], k_ref[0])
    m_old = m_ref[...]
    m_new = jnp.maximum(m_old, jnp.max(s, axis=-1, keepdims=True))
    alpha = jnp.exp(m_old - m_new)
    p = jnp.exp(s - m_new)
    l_ref[...] = alpha * l_ref[...] + jnp.sum(p, axis=-1, keepdims=True)
    acc_ref[...] = alpha * acc_ref[...] + _dot(p.astype(BF16), v_ref[0])
    m_ref[...] = m_new

    @pl.when(ki == pl.num_programs(3) - 1)
    def _():
        lv = lv_ref[...]
        lam = (jnp.exp(jnp.sum(lv[0:1] * lv[1:2], axis=-1, keepdims=True))
               - jnp.exp(jnp.sum(lv[2:3] * lv[3:4], axis=-1, keepdims=True)) + lam_init)
        o = acc_ref[...] / l_ref[...]
        o = o[:tq] - lam * o[tq:]
        o_ref[0] = (_rms(o) * gain_ref[...] * (1.0 - lam_init)).astype(BF16)


def diff_attention(q, k, v, lam_vecs, gain, lam_init, q_rows, q_off, k_rows, k_off, tq, tk):
    batch = q.shape[0]
    nq, nk = q_rows // tq, k_rows // tk
    qo, ko = q_off // tq, k_off // tk
    return pl.pallas_call(
        functools.partial(_attn_kernel, tq, lam_init),
        grid=(batch, DA_HEADS, nq, nk),
        in_specs=[pl.BlockSpec((1, tq, DA_V_DIM), lambda b, h, i, j: (b, qo + i, h)),
                  pl.BlockSpec((1, tk, DA_V_DIM), lambda b, h, i, j: (b, ko + j, h)),
                  pl.BlockSpec((1, tk, DA_V_DIM), lambda b, h, i, j: (b, ko + j, h)),
                  pl.BlockSpec(lam_vecs.shape, lambda b, h, i, j: (0, 0)),
                  pl.BlockSpec(gain.shape, lambda b, h, i, j: (0, 0))],
        out_specs=pl.BlockSpec((1, tq, DA_V_DIM), lambda b, h, i, j: (b, i, h)),
        out_shape=jax.ShapeDtypeStruct((batch, q_rows, DA_WIDTH), BF16),
        scratch_shapes=[pltpu.VMEM((2 * tq, DA_V_DIM), BF16), pltpu.VMEM((2 * tq, 1), F32),
                        pltpu.VMEM((2 * tq, 1), F32), pltpu.VMEM((2 * tq, DA_V_DIM), F32)],
        compiler_params=_params("parallel", "parallel", "parallel", "arbitrary"),
        name="diff_attention",
    )(q, k, v, lam_vecs, gain)


def _merge_kernel(n_lat_tiles, batch, has_ctx, *refs):
    if has_ctx:
        (x_ref, mod_ref, yf_ref, yfc_ref, odn_ref, z_ref, oda_ref, odac_ref, gate_ref, dng_ref,
         wf_ref, wdn_ref, wda_ref, wo_ref, o_ref) = refs
    else:
        (x_ref, mod_ref, yf_ref, odn_ref, z_ref, oda_ref, gate_ref, dng_ref,
         wf_ref, wdn_ref, wda_ref, wo_ref, o_ref) = refs
    mod = _mod_row(mod_ref, n_lat_tiles, batch)
    g1 = mod[:, 2 * D_MODEL:3 * D_MODEL]
    yf = yf_ref[0]
    oda = oda_ref[0]
    if has_ctx:
        is_ctx = pl.program_id(1) >= n_lat_tiles
        yf = jnp.where(is_ctx, yfc_ref[0], yf)
        oda = jnp.where(is_ctx, odac_ref[0], oda)
    o = odn_ref[0, 0] + odn_ref[1, 0]
    z = z_ref[0].astype(F32)
    dn_parts = []
    for hd in range(DN_HEADS):
        sl = slice(hd * DN_HEAD_DIM, (hd + 1) * DN_HEAD_DIM)
        dn_parts.append(_rms(o[:, sl]) * dng_ref[...] * _silu(z[:, sl]))
    odn = jnp.concatenate(dn_parts, axis=1).astype(BF16)
    gate = gate_ref[0].astype(F32)
    merged = (gate[:, 0:D_MODEL] * _dot(yf, wf_ref[...])
              + gate[:, D_MODEL:2 * D_MODEL] * _dot(odn, wdn_ref[...])
              + gate[:, 2 * D_MODEL:] * _dot(oda, wda_ref[...]))
    y = _dot(merged.astype(BF16), wo_ref[...])
    o_ref[0] = x_ref[0] + g1 * y


def merge_branches(x, mod, yf, yf_ctx, odn, z, oda, oda_ctx, gates, dn_gain, w_f, w_dn, w_da, w_o,
                   n_lat_tiles, n_tiles):
    batch, _, d = x.shape
    tm = ROW_TILE
    has_ctx = yf_ctx is not None
    lat_last = n_lat_tiles - 1

    def rows(width):
        return pl.BlockSpec((1, tm, width), lambda b, i: (b, i, 0))

    def lat_rows(width):
        return pl.BlockSpec((1, tm, width), lambda b, i: (b, jnp.minimum(i, lat_last), 0))

    def ctx_rows(width):
        return pl.BlockSpec((1, tm, width), lambda b, i: (b, jnp.maximum(i - n_lat_tiles, 0), 0))

    args = [x, mod, yf]
    specs = [rows(d), _resident(mod.shape), lat_rows(F_WIDTH)]
    if has_ctx:
        args.append(yf_ctx)
        specs.append(ctx_rows(F_WIDTH))
    args += [odn, z, oda]
    specs += [pl.BlockSpec((2, 1, tm, DN_WIDTH), lambda b, i: (0, b, i, 0)), rows(DN_WIDTH), lat_rows(DA_WIDTH)]
    if has_ctx:
        args.append(oda_ctx)
        specs.append(ctx_rows(DA_WIDTH))
    args += [gates, dn_gain, w_f, w_dn, w_da, w_o]
    specs += [rows(3 * D_MODEL), _resident(dn_gain.shape), _resident(w_f.shape), _resident(w_dn.shape),
              _resident(w_da.shape), _resident(w_o.shape)]
    return pl.pallas_call(
        functools.partial(_merge_kernel, n_lat_tiles, batch, has_ctx),
        grid=(batch, n_tiles),
        in_specs=specs,
        out_specs=rows(d),
        out_shape=jax.ShapeDtypeStruct((batch, n_tiles * tm, d), F32),
        compiler_params=_params("parallel", "parallel"),
        name="merge_branches",
    )(*args)


def _mlp_kernel(n_lat_tiles, batch, final, x_ref, mod_ref, gain_ref, w1_ref, w2_ref, fg_ref, o_ref):
    mod = _mod_row(mod_ref, n_lat_tiles, batch)
    sh = mod[:, 3 * D_MODEL:4 * D_MODEL]
    sc = mod[:, 4 * D_MODEL:5 * D_MODEL]
    g2 = mod[:, 5 * D_MODEL:]
    x = x_ref[0]
    h = ((_rms(x) * gain_ref[...]) * (1.0 + sc) + sh).astype(BF16)
    a = jnp.maximum(_dot(h, w1_ref[...]), 0.0)
    y = x + g2 * _dot((a * a).astype(BF16), w2_ref[...])
    if final:
        y = _rms(y) * fg_ref[...]
    o_ref[0] = y


def mlp_block(x, mod, gain, w1, w2, final_gain, n_lat_tiles, final):
    batch, t, d = x.shape
    tm = ROW_TILE
    spec = pl.BlockSpec((1, tm, d), lambda b, i: (b, i, 0))
    return pl.pallas_call(
        functools.partial(_mlp_kernel, n_lat_tiles, batch, final),
        grid=(batch, t // tm),
        in_specs=[spec, _resident(mod.shape), _resident(gain.shape), _resident(w1.shape), _resident(w2.shape),
                  _resident(final_gain.shape)],
        out_specs=spec,
        out_shape=jax.ShapeDtypeStruct((batch, t, d), F32),
        compiler_params=_params("parallel", "parallel"),
        name="mlp_block",
    )(x, mod, gain, w1, w2, final_gain)


def _rope_partner_columns(w):
    j = np.arange(w.shape[1])
    partner = j ^ 16
    sign = np.where((j & 16) == 0, -1.0, 1.0).astype(np.float32)
    return w[:, partner] * sign


def _arrange_w_in(w_in):
    splits = np.cumsum([F_WIDTH, 3 * DN_WIDTH, DN_WIDTH, 4 * DN_HEADS, DA_WIDTH, DA_WIDTH, DA_WIDTH])
    w_uf, w_qkv, w_z, w_ab, w_q, w_k, w_v, w_g = jnp.split(w_in, splits, axis=1)
    h = DN_HEADS
    pad = jnp.zeros((w_in.shape[0], 128 - 2 * h), w_in.dtype)
    ab = jnp.concatenate([w_ab[:, 0:h], w_ab[:, 2 * h:3 * h], pad,
                          w_ab[:, h:2 * h], w_ab[:, 3 * h:4 * h], pad], axis=1)
    cols = [w_uf, w_qkv, w_z, w_q, _rope_partner_columns(w_q), w_k, _rope_partner_columns(w_k), w_v, w_g, ab]
    return jnp.concatenate(cols, axis=1).astype(BF16)


def _rope_tables(n_lat, n_ctx):
    t = jnp.arange(n_lat)
    row = (t // GRID_W).astype(F32)
    col = (t % GRID_W).astype(F32)
    n_freq = DA_QK_DIM // 4
    inv_freq = ROPE_THETA ** (-jnp.arange(n_freq, dtype=F32) / n_freq)
    ang_r = row[:, None] * inv_freq[None, :]
    ang_c = col[:, None] * inv_freq[None, :]
    ang = jnp.concatenate([ang_r, ang_r, ang_c, ang_c], axis=1)
    cos = jnp.concatenate([jnp.tile(jnp.cos(ang), (1, 2)), jnp.ones((n_ctx, 128), F32)], axis=0)
    sin = jnp.concatenate([jnp.tile(jnp.sin(ang), (1, 2)), jnp.zeros((n_ctx, 128), F32)], axis=0)
    return cos, sin


def _decay_params(a_log, dt_bias):
    h = DN_HEADS
    out = jnp.zeros((8, 256), F32)
    for d in range(2):
        out = out.at[0, d * 128 + h:d * 128 + 2 * h].set(a_log[d])
        out = out.at[1, d * 128 + h:d * 128 + 2 * h].set(dt_bias[d])
    return out


def kernel(x, c, ctx, c_ctx, norm1, norm2, w_ada, b_ada, w_in, conv_w, a_log, dt_bias, dn_gain, lam_vecs,
           da_gain, w_f, w_dn, w_da, w_o, w_mlp1, w_mlp2, final_norm):
    batch, n_lat, d = x.shape
    n_ctx = ctx.shape[1]
    depth = w_in.shape[0]
    t_all = n_lat + n_ctx
    n_lat_tiles = n_lat // ROW_TILE
    n_tiles = t_all // ROW_TILE

    cvec = jnp.zeros((8, d), F32).at[:batch].set(c).at[batch].set(c_ctx)
    mods = ada_modulation(cvec, w_ada, b_ada)
    cos_t, sin_t = _rope_tables(n_lat, n_ctx)
    xs = jnp.concatenate([x, ctx], axis=1)
    final_gain = final_norm.reshape(1, d)

    for l in range(depth):
        last = l == depth - 1
        lam_init = 0.8 - 0.6 * math.exp(-0.3 * l)
        mod = mods[l]
        w_big = _arrange_w_in(w_in[l])
        abp = _decay_params(a_log[l], dt_bias[l])
        uf, qkv, z, q, k, v, gates, gb = input_projection(
            xs, mod, norm1[l].reshape(1, d), w_big, cos_t, sin_t, abp, n_lat_tiles)

        yf = fourier_latent(uf, n_lat)
        conv_w8 = jnp.zeros((8, 3 * DN_WIDTH), F32).at[:DN_CONV].set(conv_w[l])
        qn, kn, vn = deltanet_prep(qkv, conv_w8, n_lat_tiles)
        odn = deltanet_scan(qn, kn, vn, gb, n_lat)
        gain_da = da_gain[l].reshape(1, DA_V_DIM)
        tk = next(t for t in (768, 512, 256) if t_all % t == 0)
        oda = diff_attention(q, k, v, lam_vecs[l], gain_da, lam_init, n_lat, 0, t_all, 0, 256, tk)
        if last:
            yf_ctx = oda_ctx = None
            tiles = n_lat_tiles
        else:
            yf_ctx = fourier_context(uf, n_lat, n_ctx)
            oda_ctx = diff_attention(q, k, v, lam_vecs[l], gain_da, lam_init, n_ctx, n_lat, n_ctx, n_lat,
                                     n_ctx, n_ctx)
            tiles = n_tiles
        x1 = merge_branches(xs, mod, yf, yf_ctx, odn, z, oda, oda_ctx, gates, dn_gain[l].reshape(1, DN_HEAD_DIM),
                            w_f[l].astype(BF16), w_dn[l].astype(BF16), w_da[l].astype(BF16), w_o[l].astype(BF16),
                            n_lat_tiles, tiles)
        xs = mlp_block(x1, mod, norm2[l].reshape(1, d), w_mlp1[l].astype(BF16), w_mlp2[l].astype(BF16),
                       final_gain, n_lat_tiles, last)
    return xs
```

```python
import functools
import math

import numpy as np
import jax
import jax.numpy as jnp
from jax import lax
from jax.experimental import pallas as pl
from jax.experimental.pallas import tpu as pltpu

F32 = jnp.float32
BF16 = jnp.bfloat16

D_MODEL = 1024
GRID_W = 64
F_GROUPS = 4
F_GROUP_DIM = 128
F_WIDTH = F_GROUPS * F_GROUP_DIM
DN_HEADS = 4
DN_HEAD_DIM = 128
DN_WIDTH = DN_HEADS * DN_HEAD_DIM
DN_CONV = 5
DN_CHUNK = 64
DA_HEADS = 4
DA_QK_DIM = 64
DA_V_DIM = 2 * DA_QK_DIM
DA_WIDTH = DA_HEADS * DA_V_DIM
ROPE_THETA = 10000.0
D_FF = 4 * D_MODEL
NORM_EPS = 1e-6
LOG2_E = math.log2(math.e)

ROW_TILE = 256
ATTN_Q_TILE = 1024
CONV_HALO = 16
VMEM_LIMIT = 56 * 1024 * 1024

C_UF = 0
C_QKV = C_UF + F_WIDTH
C_Z = C_QKV + 3 * DN_WIDTH
C_Q = C_Z + DN_WIDTH
C_QP = C_Q + DA_WIDTH
C_K = C_QP + DA_WIDTH
C_KP = C_K + DA_WIDTH
C_V = C_KP + DA_WIDTH
C_GATE = C_V + DA_WIDTH
C_AB = C_GATE + 3 * D_MODEL
W_IN_COLS = C_AB + 256


def _dot(a, b):
    return jnp.dot(a, b, preferred_element_type=F32)


def _dot_f32(a, b):
    return jnp.dot(a, b, preferred_element_type=F32, precision=lax.Precision.HIGHEST)


def _dot_nt(a, b):
    return lax.dot_general(a, b, (((1,), (1,)), ((), ())), preferred_element_type=F32)


def _dot_tn(a, b):
    return lax.dot_general(a, b, (((0,), (0,)), ((), ())), preferred_element_type=F32)


def _silu(x):
    return x * jax.nn.sigmoid(x)


def _softplus(x):
    return jnp.maximum(x, 0.0) + jnp.log1p(jnp.exp(-jnp.abs(x)))


def _params(*sem):
    return pltpu.CompilerParams(dimension_semantics=sem, vmem_limit_bytes=VMEM_LIMIT)


def _resident(shape):
    nd = len(shape)
    return pl.BlockSpec(shape, lambda *_: (0,) * nd, pipeline_mode=pl.Buffered(1))


def _ada_kernel(c_ref, w_ref, b_ref, o_ref):
    a = _silu(c_ref[...])
    o_ref[0] = _dot_f32(a, w_ref[0]) + b_ref[0]


def ada_modulation(cvec, w_ada, b_ada):
    depth, d, n = w_ada.shape
    tn = 1536
    return pl.pallas_call(
        _ada_kernel,
        grid=(depth, n // tn),
        in_specs=[pl.BlockSpec((8, d), lambda l, j: (0, 0)),
                  pl.BlockSpec((1, d, tn), lambda l, j: (l, 0, j)),
                  pl.BlockSpec((1, 1, tn), lambda l, j: (l, 0, j))],
        out_specs=pl.BlockSpec((1, 8, tn), lambda l, j: (l, 0, j)),
        out_shape=jax.ShapeDtypeStruct((depth, 8, n), F32),
        compiler_params=_params("arbitrary", "arbitrary"),
        name="ada_modulation",
    )(cvec, w_ada, b_ada.reshape(depth, 1, n))


def _mod_row(mod_ref, n_lat_tiles, batch):
    i = pl.program_id(1)
    b = pl.program_id(0)
    r = jnp.where(i >= n_lat_tiles, batch, b)
    return mod_ref[pl.ds(r, 1), :]


def _rms(x):
    return x * lax.rsqrt(jnp.mean(x * x, axis=-1, keepdims=True) + NORM_EPS)


def _inproj_kernel(n_lat_tiles, batch, x_ref, mod_ref, gain_ref, w_ref, cos_ref, sin_ref, abp_ref,
                   uf_ref, qkv_ref, z_ref, q_ref, k_ref, v_ref, gate_ref, gb_ref):
    mod = _mod_row(mod_ref, n_lat_tiles, batch)
    sh = mod[:, 0:D_MODEL]
    sc = mod[:, D_MODEL:2 * D_MODEL]
    h = (_rms(x_ref[0]) * gain_ref[...]) * (1.0 + sc) + sh
    hb = h.astype(BF16)

    def proj(c0, width):
        return _dot(hb, w_ref[:, c0:c0 + width])

    uf_ref[0] = proj(C_UF, F_WIDTH).astype(BF16)
    for j in range(3):
        qkv_ref[0, :, j * DN_WIDTH:(j + 1) * DN_WIDTH] = proj(C_QKV + j * DN_WIDTH, DN_WIDTH).astype(BF16)
    z_ref[0] = proj(C_Z, DN_WIDTH).astype(BF16)
    cos = jnp.tile(cos_ref[...], (1, DA_HEADS))
    sin = jnp.tile(sin_ref[...], (1, DA_HEADS))
    q = proj(C_Q, DA_WIDTH) * cos + proj(C_QP, DA_WIDTH) * sin
    q_ref[0] = (q * (DA_QK_DIM ** -0.5 * LOG2_E)).astype(BF16)
    k = proj(C_K, DA_WIDTH) * cos + proj(C_KP, DA_WIDTH) * sin
    k_ref[0] = k.astype(BF16)
    v_ref[0] = proj(C_V, DA_WIDTH).astype(BF16)
    for j in range(3 * D_MODEL // 512):
        g = proj(C_GATE + j * 512, 512)
        gate_ref[0, :, j * 512:(j + 1) * 512] = jax.nn.sigmoid(g).astype(BF16)
    ab = proj(C_AB, 256)
    abp = abp_ref[...]
    lane = lax.broadcasted_iota(jnp.int32, ab.shape, 1) % 128
    beta = jax.nn.sigmoid(ab)
    g = -jnp.exp(abp[0:1, :]) * _softplus(ab + abp[1:2, :])
    gb_ref[0] = jnp.where(lane < DN_HEADS, beta, g)


def input_projection(x, mod, gain, w_big, cos_t, sin_t, abp, n_lat_tiles):
    batch, t_all, d = x.shape
    n_tiles = t_all // ROW_TILE
    tm = ROW_TILE

    def rows(width, dtype):
        return (pl.BlockSpec((1, tm, width), lambda b, i: (b, i, 0)),
                jax.ShapeDtypeStruct((batch, t_all, width), dtype))

    outs = [rows(F_WIDTH, BF16), rows(3 * DN_WIDTH, BF16), rows(DN_WIDTH, BF16), rows(DA_WIDTH, BF16),
            rows(DA_WIDTH, BF16), rows(DA_WIDTH, BF16), rows(3 * D_MODEL, BF16), rows(256, F32)]
    return pl.pallas_call(
        functools.partial(_inproj_kernel, n_lat_tiles, batch),
        grid=(batch, n_tiles),
        in_specs=[pl.BlockSpec((1, tm, d), lambda b, i: (b, i, 0)),
                  _resident(mod.shape), _resident(gain.shape), _resident(w_big.shape),
                  pl.BlockSpec((tm, 128), lambda b, i: (i, 0)),
                  pl.BlockSpec((tm, 128), lambda b, i: (i, 0)),
                  _resident(abp.shape)],
        out_specs=[o[0] for o in outs],
        out_shape=[o[1] for o in outs],
        compiler_params=_params("parallel", "parallel"),
        name="input_projection",
    )(x, mod, gain, w_big, cos_t, sin_t, abp)


def _dft_tables(n_pos):
    n1 = n_pos // GRID_W
    n2 = GRID_W
    a = np.arange(n1)
    ang1 = 2.0 * np.pi * ((a[:, None] * a[None, :]) % n1) / n1
    stage1 = np.concatenate([np.cos(ang1), -np.sin(ang1)], axis=0)
    k1 = np.arange(n1)[:, None, None]
    k2 = np.arange(n2)[None, :, None]
    m = np.arange(n2)[None, None, :]
    ang2 = 2.0 * np.pi * ((m * (k1 + n1 * k2)) % n_pos) / n_pos
    c2, s2 = np.cos(ang2), np.sin(ang2)
    stage2 = np.concatenate([np.concatenate([c2, s2], axis=2),
                             np.concatenate([-s2, c2], axis=2)], axis=1)
    return stage1, stage2


def _channel_table(n_pos):
    c = np.arange(F_GROUP_DIM)
    ang = 2.0 * np.pi * ((c[:, None] * c[None, :]) % F_GROUP_DIM) / F_GROUP_DIM
    scale = 1.0 / math.sqrt(n_pos * F_GROUP_DIM)
    return np.concatenate([np.cos(ang), np.sin(ang)], axis=0) * scale


def _channel_mix(pr, pi, chan):
    outs = []
    for g in range(F_GROUPS):
        sl = slice(g * F_GROUP_DIM, (g + 1) * F_GROUP_DIM)
        pg = jnp.concatenate([pr[:, sl], pi[:, sl]], axis=1).astype(BF16)
        outs.append(_dot(pg, chan))
    return jnp.concatenate(outs, axis=1)


def _fft1_kernel(n1, f_ref, x_ref, ar_ref, ai_ref):
    a = _dot(f_ref[...].astype(BF16), x_ref[0])
    ar_ref[0] = a[:n1].astype(BF16)
    ai_ref[0] = a[n1:].astype(BF16)


def _fft2_kernel(kb, m_ref, ar_ref, ai_ref, chan_ref, o_ref):
    chan = chan_ref[...].astype(BF16)
    for j in range(kb):
        a = jnp.concatenate([ar_ref[0, j], ai_ref[0, j]], axis=0)
        p = _dot(m_ref[j].astype(BF16), a)
        y = _channel_mix(p[:GRID_W], p[GRID_W:], chan)
        o_ref[0, :, j * F_WIDTH:(j + 1) * F_WIDTH] = y.astype(BF16)


def fourier_latent(uf_all, n_pos):
    batch, t_all, _ = uf_all.shape
    n1 = n_pos // GRID_W
    stage1, stage2 = _dft_tables(n_pos)
    f1 = jnp.asarray(stage1, F32)
    m2 = jnp.asarray(stage2, F32)
    chan = jnp.asarray(_channel_table(n_pos), F32)
    wide = GRID_W * F_WIDTH
    x2 = uf_all.reshape(batch, t_all // GRID_W, wide)
    tn = 4096
    ar, ai = pl.pallas_call(
        functools.partial(_fft1_kernel, n1),
        grid=(batch, wide // tn),
        in_specs=[_resident(f1.shape),
                  pl.BlockSpec((1, n1, tn), lambda b, j: (b, 0, j))],
        out_specs=[pl.BlockSpec((1, n1, tn), lambda b, j: (b, 0, j))] * 2,
        out_shape=[jax.ShapeDtypeStruct((batch, n1, wide), BF16)] * 2,
        compiler_params=_params("parallel", "parallel"),
        name="fourier_stage1",
    )(f1, x2)
    ar = ar.reshape(batch, n1, GRID_W, F_WIDTH)
    ai = ai.reshape(batch, n1, GRID_W, F_WIDTH)
    kb = 8
    y = pl.pallas_call(
        functools.partial(_fft2_kernel, kb),
        grid=(batch, n1 // kb),
        in_specs=[pl.BlockSpec((kb, 2 * GRID_W, 2 * GRID_W), lambda b, i: (i, 0, 0)),
                  pl.BlockSpec((1, kb, GRID_W, F_WIDTH), lambda b, i: (b, i, 0, 0)),
                  pl.BlockSpec((1, kb, GRID_W, F_WIDTH), lambda b, i: (b, i, 0, 0)),
                  _resident(chan.shape)],
        out_specs=pl.BlockSpec((1, GRID_W, kb * F_WIDTH), lambda b, i: (b, 0, i)),
        out_shape=jax.ShapeDtypeStruct((batch, GRID_W, n1 * F_WIDTH), BF16),
        compiler_params=_params("parallel", "parallel"),
        name="fourier_stage2",
    )(m2, ar, ai, chan)
    return y.reshape(batch, n_pos, F_WIDTH)


def _fft_ctx_kernel(tc, f_ref, x_ref, chan_ref, o_ref):
    p = _dot(f_ref[...].astype(BF16), x_ref[0])
    o_ref[0] = _channel_mix(p[:tc], p[tc:], chan_ref[...].astype(BF16)).astype(BF16)


def fourier_context(uf_all, n_pos, tc):
    batch = uf_all.shape[0]
    a = np.arange(tc)
    ang = 2.0 * np.pi * ((a[:, None] * a[None, :]) % tc) / tc
    f = jnp.asarray(np.concatenate([np.cos(ang), -np.sin(ang)], axis=0), F32)
    chan = jnp.asarray(_channel_table(tc), F32)
    return pl.pallas_call(
        functools.partial(_fft_ctx_kernel, tc),
        grid=(batch,),
        in_specs=[_resident(f.shape),
                  pl.BlockSpec((1, tc, F_WIDTH), lambda b: (b, n_pos // tc, 0)),
                  _resident(chan.shape)],
        out_specs=pl.BlockSpec((1, tc, F_WIDTH), lambda b: (b, 0, 0)),
        out_shape=jax.ShapeDtypeStruct((batch, tc, F_WIDTH), BF16),
        compiler_params=_params("parallel"),
        name="fourier_context",
    )(f, uf_all, chan)


def _dnprep_kernel(n_lat_tiles, n_tiles, cur_ref, prev_ref, next_ref, w_ref, q_ref, k_ref, v_ref, ext_ref):
    i = pl.program_id(1)
    tm = ROW_TILE
    h = CONV_HALO
    prev_ok = jnp.logical_and(i != 0, i != n_lat_tiles)
    next_ok = jnp.logical_and(i != n_lat_tiles - 1, i != n_tiles - 1)
    ext_ref[0:h, :] = jnp.where(prev_ok, prev_ref[0].astype(F32), 0.0)
    ext_ref[h:h + tm, :] = cur_ref[0].astype(F32)
    ext_ref[h + tm:h + tm + h, :] = jnp.where(next_ok, next_ref[0].astype(F32), 0.0)
    pad = DN_CONV // 2
    acc = None
    for j in range(DN_CONV):
        term = ext_ref[h - pad + j:h - pad + j + tm, :] * w_ref[j:j + 1, :]
        acc = term if acc is None else acc + term
    u = _silu(acc)

    def l2n(block):
        return block * lax.rsqrt(jnp.sum(block * block, axis=-1, keepdims=True) + NORM_EPS)

    for hd in range(DN_HEADS):
        sl = slice(hd * DN_HEAD_DIM, (hd + 1) * DN_HEAD_DIM)
        q = u[:, hd * DN_HEAD_DIM:(hd + 1) * DN_HEAD_DIM]
        k = u[:, DN_WIDTH + hd * DN_HEAD_DIM:DN_WIDTH + (hd + 1) * DN_HEAD_DIM]
        q_ref[0, :, sl] = (l2n(q) * (DN_HEAD_DIM ** -0.5)).astype(BF16)
        k_ref[0, :, sl] = l2n(k).astype(BF16)
    v_ref[0] = u[:, 2 * DN_WIDTH:].astype(BF16)


def deltanet_prep(qkv, conv_w8, n_lat_tiles):
    batch, t_all, width = qkv.shape
    tm = ROW_TILE
    n_tiles = t_all // tm
    per = tm // CONV_HALO
    last = t_all // CONV_HALO - 1
    out = (pl.BlockSpec((1, tm, DN_WIDTH), lambda b, i: (b, i, 0)),
           jax.ShapeDtypeStruct((batch, t_all, DN_WIDTH), BF16))
    return pl.pallas_call(
        functools.partial(_dnprep_kernel, n_lat_tiles, n_tiles),
        grid=(batch, n_tiles),
        in_specs=[pl.BlockSpec((1, tm, width), lambda b, i: (b, i, 0)),
                  pl.BlockSpec((1, CONV_HALO, width), lambda b, i: (b, jnp.maximum(i * per - 1, 0), 0)),
                  pl.BlockSpec((1, CONV_HALO, width), lambda b, i: (b, jnp.minimum((i + 1) * per, last), 0)),
                  _resident(conv_w8.shape)],
        out_specs=[out[0]] * 3,
        out_shape=[out[1]] * 3,
        scratch_shapes=[pltpu.VMEM((tm + 2 * CONV_HALO, width), F32)],
        compiler_params=_params("parallel", "parallel"),
        name="deltanet_prep",
    )(qkv, qkv, qkv, conv_w8)


def _dnscan_kernel(batch, qf_ref, kf_ref, vf_ref, gf_ref, qb_ref, kb_ref, vb_ref, gb_ref, of_ref, ob_ref, s_ref):
    c = DN_CHUNK
    hh = DN_HEADS
    chains = [(d, b, h) for d in range(2) for b in range(batch) for h in range(hh)]
    n = len(chains)
    half = n // 2

    @pl.when(pl.program_id(0) == 0)
    def _():
        s_ref[...] = jnp.zeros_like(s_ref)

    i2 = lax.broadcasted_iota(jnp.int32, (c, c), 0)
    j2 = lax.broadcasted_iota(jnp.int32, (c, c), 1)
    tri = [(j2 <= i2).astype(F32), (j2 >= i2).astype(F32)]
    g_refs = (gf_ref, gb_ref)
    gcum, grow, gtot, graw = {}, {}, {}, {}
    for d in range(2):
        for b in range(batch):
            g = g_refs[d][b]
            graw[d, b] = g
            gcum[d, b] = _dot_f32(tri[d], g)
            grow[d, b] = lax.dot_general(g.T, tri[d], (((1,), (1,)), ((), ())), preferred_element_type=F32,
                                         precision=lax.Precision.HIGHEST)
            gtot[d, b] = jnp.sum(g, axis=0, keepdims=True)

    def per_chain(fn):
        return jnp.stack([fn(d, b, h) for d, b, h in chains])

    def tile(refs):
        return per_chain(lambda d, b, h: refs[d][b, :, h * DN_HEAD_DIM:(h + 1) * DN_HEAD_DIM]).astype(F32)

    q = tile((qf_ref, qb_ref))
    k = tile((kf_ref, kb_ref))
    v = tile((vf_ref, vb_ref))
    beta = per_chain(lambda d, b, h: graw[d, b][:, h:h + 1])
    gc = per_chain(lambda d, b, h: gcum[d, b][:, hh + h:hh + h + 1])
    gr = per_chain(lambda d, b, h: grow[d, b][hh + h:hh + h + 1, :])
    gl = per_chain(lambda d, b, h: gtot[d, b][:, hh + h:hh + h + 1])

    def bmm(a, bm, fn=_dot):
        return jnp.stack([fn(a[m], bm[m]) for m in range(n)])

    ch = lax.broadcasted_iota(jnp.int32, (n, c, c), 0)
    ii = lax.broadcasted_iota(jnp.int32, (n, c, c), 1)
    jj = lax.broadcasted_iota(jnp.int32, (n, c, c), 2)
    diff = jnp.where(ch < half, ii - jj, jj - ii)
    incl = diff >= 0
    strict = diff > 0
    eye = (ii == jj).astype(F32)

    decay = jnp.exp(jnp.where(incl, gc - gr, -jnp.inf))
    kbeta = k * beta
    k16 = k.astype(BF16)
    lmat = jnp.where(strict, bmm(kbeta.astype(BF16), k16, _dot_nt) * decay, 0.0)
    blk = lambda s: (ii // s) == (jj // s)
    t = eye - jnp.where(blk(2), lmat, 0.0)
    s = 2
    while s < c:
        ls = jnp.where(jnp.logical_and(blk(2 * s), jnp.logical_not(blk(s))), lmat, 0.0)
        t16 = t.astype(BF16)
        t = t - bmm(bmm(t16, ls.astype(BF16)).astype(BF16), t16)
        s *= 2
    eg = jnp.exp(gc)
    rhs = jnp.concatenate([v * beta, kbeta * eg], axis=2).astype(BF16)
    uw = bmm(t.astype(BF16), rhs)
    u = uw[:, :, :DN_HEAD_DIM]
    w = uw[:, :, DN_HEAD_DIM:]
    state = s_ref[...]
    s16 = state.astype(BF16)
    ws = bmm(jnp.concatenate([w, q * eg], axis=1).astype(BF16), s16)
    v_new = u - ws[:, :c]
    qk = bmm(q.astype(BF16), k16, _dot_nt) * decay
    v16 = v_new.astype(BF16)
    out = ws[:, c:] + bmm(qk.astype(BF16), v16)
    ke = (k * jnp.exp(gl - gc)).astype(BF16)
    s_ref[...] = state * jnp.exp(gl) + bmm(ke, v16, _dot_tn)
    o_refs = (of_ref, ob_ref)
    for m, (d, b, h) in enumerate(chains):
        o_refs[d][b, :, h * DN_HEAD_DIM:(h + 1) * DN_HEAD_DIM] = out[m]


def deltanet_scan(qn, kn, vn, gb, n_lat):
    batch, t_all, _ = qn.shape
    c = DN_CHUNK
    n_chunks = t_all // c
    n_lat_chunks = n_lat // c
    n_ctx_chunks = n_chunks - n_lat_chunks

    def fwd(s):
        return jnp.where(s < n_ctx_chunks, n_lat_chunks + s, s - n_ctx_chunks)

    def bwd(s):
        return n_chunks - 1 - s

    tok_f = pl.BlockSpec((batch, c, DN_WIDTH), lambda s: (0, fwd(s), 0))
    tok_b = pl.BlockSpec((batch, c, DN_WIDTH), lambda s: (0, bwd(s), 0))
    out = jax.ShapeDtypeStruct((batch, t_all, DN_WIDTH), F32)
    return pl.pallas_call(
        functools.partial(_dnscan_kernel, batch),
        grid=(n_chunks,),
        in_specs=[tok_f, tok_f, tok_f, pl.BlockSpec((batch, c, 128), lambda s: (0, fwd(s), 0)),
                  tok_b, tok_b, tok_b, pl.BlockSpec((batch, c, 128), lambda s: (0, bwd(s), 1))],
        out_specs=[tok_f, tok_b],
        out_shape=[out, out],
        scratch_shapes=[pltpu.VMEM((2 * batch * DN_HEADS, DN_HEAD_DIM, DN_HEAD_DIM), F32)],
        compiler_params=_params("arbitrary"),
        name="deltanet_scan",
    )(qn, kn, vn, gb, qn, kn, vn, gb)


def _attn_kernel(tq, tk, nk, lam_init, q_ref, k_ref, v_ref, lv_ref, gain_ref, o_ref,
                 qs_ref, vp_ref, s0_ref, s1_ref, m_ref, acc_ref):
    dv = DA_V_DIM

    @pl.when(pl.program_id(2) == 0)
    def _():
        vp_ref[:, 0:dv] = v_ref[0]
        vp_ref[:, dv:2 * dv] = jnp.ones((vp_ref.shape[0], dv), BF16)

    q = q_ref[0]
    lane = lax.broadcasted_iota(jnp.int32, q.shape, 1)
    zero = jnp.zeros_like(q)
    qs_ref[0:tq, :] = jnp.where(lane < DA_QK_DIM, q, zero)
    qs_ref[tq:2 * tq, :] = jnp.where(lane >= DA_QK_DIM, q, zero)
    m_ref[...] = jnp.full_like(m_ref, -jnp.inf)
    acc_ref[...] = jnp.zeros_like(acc_ref)
    s_refs = (s0_ref, s1_ref)
    s0_ref[...] = _dot_nt(qs_ref[...], k_ref[0, 0:tk, :])

    def step(j, cur, prefetch):
        if prefetch:
            k_next = k_ref[0, pl.ds(pl.multiple_of((j + 1) * tk, tk), tk), :]
            s_refs[1 - cur][...] = _dot_nt(qs_ref[...], k_next)
        s = s_refs[cur][...]
        m_old = m_ref[...]
        m_new = jnp.maximum(m_old, jnp.max(s, axis=-1, keepdims=True))
        alpha = jnp.exp2(m_old - m_new)
        p = jnp.exp2(s - m_new).astype(BF16)
        vp = vp_ref[pl.ds(pl.multiple_of(j * tk, tk), tk), :]
        acc_ref[...] = alpha * acc_ref[...] + _dot(p, vp)
        m_ref[...] = m_new

    n_pairs = (nk - 1) // 2
    if n_pairs > 0:
        def body(i, carry):
            step(2 * i, 0, True)
            step(2 * i + 1, 1, True)
            return carry
        lax.fori_loop(0, n_pairs, body, 0)
    for j in range(2 * n_pairs, nk):
        step(j, j % 2, j + 1 < nk)

    lv = lv_ref[...]
    lam = (jnp.exp(jnp.sum(lv[0:1] * lv[1:2], axis=-1, keepdims=True))
           - jnp.exp(jnp.sum(lv[2:3] * lv[3:4], axis=-1, keepdims=True)) + lam_init)
    acc = acc_ref[...]
    o = acc[:, :dv] / acc[:, dv:]
    o = o[:tq] - lam * o[tq:]
    o_ref[0] = (_rms(o) * gain_ref[...] * (1.0 - lam_init)).astype(BF16)


def diff_attention(q, k, v, lam_vecs, gain, lam_init, q_rows, q_off, k_rows, k_off, tq, tk):
    batch = q.shape[0]
    nq, nk = q_rows // tq, k_rows // tk
    qo, ko = q_off // tq, k_off // k_rows
    kv = pl.BlockSpec((1, k_rows, DA_V_DIM), lambda b, h, i: (b, ko, h))
    return pl.pallas_call(
        functools.partial(_attn_kernel, tq, tk, nk, lam_init),
        grid=(batch, DA_HEADS, nq),
        in_specs=[pl.BlockSpec((1, tq, DA_V_DIM), lambda b, h, i: (b, qo + i, h)), kv, kv,
                  pl.BlockSpec(lam_vecs.shape, lambda b, h, i: (0, 0)),
                  pl.BlockSpec(gain.shape, lambda b, h, i: (0, 0))],
        out_specs=pl.BlockSpec((1, tq, DA_V_DIM), lambda b, h, i: (b, i, h)),
        out_shape=jax.ShapeDtypeStruct((batch, q_rows, DA_WIDTH), BF16),
        scratch_shapes=[pltpu.VMEM((2 * tq, DA_V_DIM), BF16), pltpu.VMEM((k_rows, 2 * DA_V_DIM), BF16),
                        pltpu.VMEM((2 * tq, tk), F32), pltpu.VMEM((2 * tq, tk), F32), pltpu.VMEM((2 * tq, 1), F32),
                        pltpu.VMEM((2 * tq, 2 * DA_V_DIM), F32)],
        compiler_params=_params("parallel", "parallel", "arbitrary"),
        name="diff_attention",
    )(q, k, v, lam_vecs, gain)


def _merge_kernel(n_lat_tiles, batch, has_ctx, *refs):
    if has_ctx:
        (x_ref, mod_ref, yf_ref, yfc_ref, odnf_ref, odnb_ref, z_ref, oda_ref, odac_ref, gate_ref, dng_ref,
         wf_ref, wdn_ref, wda_ref, wo_ref, o_ref) = refs
    else:
        (x_ref, mod_ref, yf_ref, odnf_ref, odnb_ref, z_ref, oda_ref, gate_ref, dng_ref,
         wf_ref, wdn_ref, wda_ref, wo_ref, o_ref) = refs
    mod = _mod_row(mod_ref, n_lat_tiles, batch)
    g1 = mod[:, 2 * D_MODEL:3 * D_MODEL]
    yf = yf_ref[0]
    oda = oda_ref[0]
    if has_ctx:
        is_ctx = pl.program_id(1) >= n_lat_tiles
        yf = jnp.where(is_ctx, yfc_ref[0], yf)
        oda = jnp.where(is_ctx, odac_ref[0], oda)
    o = odnf_ref[0] + odnb_ref[0]
    z = z_ref[0].astype(F32)
    dn_parts = []
    for hd in range(DN_HEADS):
        sl = slice(hd * DN_HEAD_DIM, (hd + 1) * DN_HEAD_DIM)
        dn_parts.append(_rms(o[:, sl]) * dng_ref[...] * _silu(z[:, sl]))
    odn = jnp.concatenate(dn_parts, axis=1).astype(BF16)
    gate = gate_ref[0].astype(F32)
    merged = (gate[:, 0:D_MODEL] * _dot(yf, wf_ref[...])
              + gate[:, D_MODEL:2 * D_MODEL] * _dot(odn, wdn_ref[...])
              + gate[:, 2 * D_MODEL:] * _dot(oda, wda_ref[...]))
    y = _dot(merged.astype(BF16), wo_ref[...])
    o_ref[0] = x_ref[0] + g1 * y


def merge_branches(x, mod, yf, yf_ctx, odn, z, oda, oda_ctx, gates, dn_gain, w_f, w_dn, w_da, w_o,
                   n_lat_tiles, n_tiles):
    batch, _, d = x.shape
    tm = ROW_TILE
    has_ctx = yf_ctx is not None
    lat_last = n_lat_tiles - 1

    def rows(width):
        return pl.BlockSpec((1, tm, width), lambda b, i: (b, i, 0))

    def lat_rows(width):
        return pl.BlockSpec((1, tm, width), lambda b, i: (b, jnp.minimum(i, lat_last), 0))

    def ctx_rows(width):
        return pl.BlockSpec((1, tm, width), lambda b, i: (b, jnp.maximum(i - n_lat_tiles, 0), 0))

    args = [x, mod, yf]
    specs = [rows(d), _resident(mod.shape), lat_rows(F_WIDTH)]
    if has_ctx:
        args.append(yf_ctx)
        specs.append(ctx_rows(F_WIDTH))
    args += [odn[0], odn[1], z, oda]
    specs += [rows(DN_WIDTH), rows(DN_WIDTH), rows(DN_WIDTH), lat_rows(DA_WIDTH)]
    if has_ctx:
        args.append(oda_ctx)
        specs.append(ctx_rows(DA_WIDTH))
    args += [gates, dn_gain, w_f, w_dn, w_da, w_o]
    specs += [rows(3 * D_MODEL), _resident(dn_gain.shape), _resident(w_f.shape), _resident(w_dn.shape),
              _resident(w_da.shape), _resident(w_o.shape)]
    return pl.pallas_call(
        functools.partial(_merge_kernel, n_lat_tiles, batch, has_ctx),
        grid=(batch, n_tiles),
        in_specs=specs,
        out_specs=rows(d),
        out_shape=jax.ShapeDtypeStruct((batch, n_tiles * tm, d), F32),
        compiler_params=_params("parallel", "parallel"),
        name="merge_branches",
    )(*args)


def _mlp_kernel(n_lat_tiles, batch, final, x_ref, mod_ref, gain_ref, w1_ref, w2_ref, fg_ref, o_ref):
    mod = _mod_row(mod_ref, n_lat_tiles, batch)
    sh = mod[:, 3 * D_MODEL:4 * D_MODEL]
    sc = mod[:, 4 * D_MODEL:5 * D_MODEL]
    g2 = mod[:, 5 * D_MODEL:]
    x = x_ref[0]
    h = ((_rms(x) * gain_ref[...]) * (1.0 + sc) + sh).astype(BF16)
    a = jnp.maximum(_dot(h, w1_ref[...]), 0.0)
    y = x + g2 * _dot((a * a).astype(BF16), w2_ref[...])
    if final:
        y = _rms(y) * fg_ref[...]
    o_ref[0] = y


def mlp_block(x, mod, gain, w1, w2, final_gain, n_lat_tiles, final):
    batch, t, d = x.shape
    tm = ROW_TILE
    spec = pl.BlockSpec((1, tm, d), lambda b, i: (b, i, 0))
    return pl.pallas_call(
        functools.partial(_mlp_kernel, n_lat_tiles, batch, final),
        grid=(batch, t // tm),
        in_specs=[spec, _resident(mod.shape), _resident(gain.shape), _resident(w1.shape), _resident(w2.shape),
                  _resident(final_gain.shape)],
        out_specs=spec,
        out_shape=jax.ShapeDtypeStruct((batch, t, d), F32),
        compiler_params=_params("parallel", "parallel"),
        name="mlp_block",
    )(x, mod, gain, w1, w2, final_gain)


def _rope_partner_columns(w):
    j = np.arange(w.shape[1])
    partner = j ^ 16
    sign = np.where((j & 16) == 0, -1.0, 1.0).astype(np.float32)
    return w[:, partner] * sign


def _arrange_w_in(w_in):
    splits = np.cumsum([F_WIDTH, 3 * DN_WIDTH, DN_WIDTH, 4 * DN_HEADS, DA_WIDTH, DA_WIDTH, DA_WIDTH])
    w_uf, w_qkv, w_z, w_ab, w_q, w_k, w_v, w_g = jnp.split(w_in, splits, axis=1)
    h = DN_HEADS
    pad = jnp.zeros((w_in.shape[0], 128 - 2 * h), w_in.dtype)
    ab = jnp.concatenate([w_ab[:, 0:h], w_ab[:, 2 * h:3 * h], pad,
                          w_ab[:, h:2 * h], w_ab[:, 3 * h:4 * h], pad], axis=1)
    cols = [w_uf, w_qkv, w_z, w_q, _rope_partner_columns(w_q), w_k, _rope_partner_columns(w_k), w_v, w_g, ab]
    return jnp.concatenate(cols, axis=1).astype(BF16)


def _rope_tables(n_lat, n_ctx):
    t = jnp.arange(n_lat)
    row = (t // GRID_W).astype(F32)
    col = (t % GRID_W).astype(F32)
    n_freq = DA_QK_DIM // 4
    inv_freq = ROPE_THETA ** (-jnp.arange(n_freq, dtype=F32) / n_freq)
    ang_r = row[:, None] * inv_freq[None, :]
    ang_c = col[:, None] * inv_freq[None, :]
    ang = jnp.concatenate([ang_r, ang_r, ang_c, ang_c], axis=1)
    cos = jnp.concatenate([jnp.tile(jnp.cos(ang), (1, 2)), jnp.ones((n_ctx, 128), F32)], axis=0)
    sin = jnp.concatenate([jnp.tile(jnp.sin(ang), (1, 2)), jnp.zeros((n_ctx, 128), F32)], axis=0)
    return cos, sin


def _decay_params(a_log, dt_bias):
    h = DN_HEADS
    out = jnp.zeros((8, 256), F32)
    for d in range(2):
        out = out.at[0, d * 128 + h:d * 128 + 2 * h].set(a_log[d])
        out = out.at[1, d * 128 + h:d * 128 + 2 * h].set(dt_bias[d])
    return out


def kernel(x, c, ctx, c_ctx, norm1, norm2, w_ada, b_ada, w_in, conv_w, a_log, dt_bias, dn_gain, lam_vecs,
           da_gain, w_f, w_dn, w_da, w_o, w_mlp1, w_mlp2, final_norm):
    batch, n_lat, d = x.shape
    n_ctx = ctx.shape[1]
    depth = w_in.shape[0]
    t_all = n_lat + n_ctx
    n_lat_tiles = n_lat // ROW_TILE
    n_tiles = t_all // ROW_TILE

    cvec = jnp.zeros((8, d), F32).at[:batch].set(c).at[batch].set(c_ctx)
    mods = ada_modulation(cvec, w_ada, b_ada)
    cos_t, sin_t = _rope_tables(n_lat, n_ctx)
    xs = jnp.concatenate([x, ctx], axis=1)
    final_gain = final_norm.reshape(1, d)

    for l in range(depth):
        last = l == depth - 1
        lam_init = 0.8 - 0.6 * math.exp(-0.3 * l)
        mod = mods[l]
        w_big = _arrange_w_in(w_in[l])
        abp = _decay_params(a_log[l], dt_bias[l])
        uf, qkv, z, q, k, v, gates, gb = input_projection(
            xs, mod, norm1[l].reshape(1, d), w_big, cos_t, sin_t, abp, n_lat_tiles)

        yf = fourier_latent(uf, n_lat)
        conv_w8 = jnp.zeros((8, 3 * DN_WIDTH), F32).at[:DN_CONV].set(conv_w[l])
        qn, kn, vn = deltanet_prep(qkv, conv_w8, n_lat_tiles)
        odn = deltanet_scan(qn, kn, vn, gb, n_lat)
        gain_da = da_gain[l].reshape(1, DA_V_DIM)
        tk = next(t for t in (768, 512, 256) if t_all % t == 0)
        tq = min(ATTN_Q_TILE, n_lat)
        oda = diff_attention(q, k, v, lam_vecs[l], gain_da, lam_init, n_lat, 0, t_all, 0, tq, tk)
        if last:
            yf_ctx = oda_ctx = None
            tiles = n_lat_tiles
        else:
            yf_ctx = fourier_context(uf, n_lat, n_ctx)
            oda_ctx = diff_attention(q, k, v, lam_vecs[l], gain_da, lam_init, n_ctx, n_lat, n_ctx, n_lat,
                                     n_ctx, n_ctx)
            tiles = n_tiles
        x1 = merge_branches(xs, mod, yf, yf_ctx, odn, z, oda, oda_ctx, gates, dn_gain[l].reshape(1, DN_HEAD_DIM),
                            w_f[l].astype(BF16), w_dn[l].astype(BF16), w_da[l].astype(BF16), w_o[l].astype(BF16),
                            n_lat_tiles, tiles)
        xs = mlp_block(x1, mod, norm2[l].reshape(1, d), w_mlp1[l].astype(BF16), w_mlp2[l].astype(BF16),
                       final_gain, n_lat_tiles, last)
    return xs
```

```python
import functools
import math

import numpy as np
import jax
import jax.numpy as jnp
from jax import lax
from jax.experimental import pallas as pl
from jax.experimental.pallas import tpu as pltpu

F32 = jnp.float32
BF16 = jnp.bfloat16

D_MODEL = 1024
GRID_W = 64
F_GROUPS = 4
F_GROUP_DIM = 128
F_WIDTH = F_GROUPS * F_GROUP_DIM
DN_HEADS = 4
DN_HEAD_DIM = 128
DN_WIDTH = DN_HEADS * DN_HEAD_DIM
DN_CONV = 5
DN_CHUNK = 64
DA_HEADS = 4
DA_QK_DIM = 64
DA_V_DIM = 2 * DA_QK_DIM
DA_WIDTH = DA_HEADS * DA_V_DIM
ROPE_THETA = 10000.0
D_FF = 4 * D_MODEL
NORM_EPS = 1e-6
LOG2_E = math.log2(math.e)

ROW_TILE = 256
ATTN_Q_TILE = 1024
CONV_HALO = 16
VMEM_LIMIT = 56 * 1024 * 1024

C_UF = 0
C_QKV = C_UF + F_WIDTH
C_Z = C_QKV + 3 * DN_WIDTH
C_Q = C_Z + DN_WIDTH
C_K = C_Q + DA_WIDTH
C_V = C_K + DA_WIDTH
C_GATE = C_V + DA_WIDTH
C_AB = C_GATE + 3 * D_MODEL
W_IN_COLS = C_AB + 256


def _dot(a, b):
    return jnp.dot(a, b, preferred_element_type=F32)


def _dot_f32(a, b):
    return jnp.dot(a, b, preferred_element_type=F32, precision=lax.Precision.HIGHEST)


def _dot_nt(a, b):
    return lax.dot_general(a, b, (((1,), (1,)), ((), ())), preferred_element_type=F32)


def _dot_tn(a, b):
    return lax.dot_general(a, b, (((0,), (0,)), ((), ())), preferred_element_type=F32)


def _silu(x):
    return x * jax.nn.sigmoid(x)


def _softplus(x):
    return jnp.maximum(x, 0.0) + jnp.log1p(jnp.exp(-jnp.abs(x)))


def _params(*sem):
    return pltpu.CompilerParams(dimension_semantics=sem, vmem_limit_bytes=VMEM_LIMIT)


def _resident(shape):
    nd = len(shape)
    return pl.BlockSpec(shape, lambda *_: (0,) * nd, pipeline_mode=pl.Buffered(1))


def _ada_kernel(c_ref, w_ref, b_ref, o_ref):
    a = _silu(c_ref[...])
    o_ref[0] = _dot_f32(a, w_ref[0]) + b_ref[0]


def ada_modulation(cvec, w_ada, b_ada):
    depth, d, n = w_ada.shape
    tn = 1536
    return pl.pallas_call(
        _ada_kernel,
        grid=(depth, n // tn),
        in_specs=[pl.BlockSpec((8, d), lambda l, j: (0, 0)),
                  pl.BlockSpec((1, d, tn), lambda l, j: (l, 0, j)),
                  pl.BlockSpec((1, 1, tn), lambda l, j: (l, 0, j))],
        out_specs=pl.BlockSpec((1, 8, tn), lambda l, j: (l, 0, j)),
        out_shape=jax.ShapeDtypeStruct((depth, 8, n), F32),
        compiler_params=_params("arbitrary", "arbitrary"),
        name="ada_modulation",
    )(cvec, w_ada, b_ada.reshape(depth, 1, n))


def _mod_row(mod_ref, n_lat_tiles, batch):
    i = pl.program_id(1)
    b = pl.program_id(0)
    r = jnp.where(i >= n_lat_tiles, batch, b)
    return mod_ref[pl.ds(r, 1), :]


def _rms(x):
    return x * lax.rsqrt(jnp.mean(x * x, axis=-1, keepdims=True) + NORM_EPS)


def _token_specs(arrays, n_lat_tiles, width):
    tm = ROW_TILE
    if len(arrays) == 1:
        return [pl.BlockSpec((1, tm, width), lambda b, i: (b, i, 0))]
    return [pl.BlockSpec((1, tm, width), lambda b, i: (b, jnp.minimum(i, n_lat_tiles - 1), 0)),
            pl.BlockSpec((1, tm, width), lambda b, i: (b, jnp.maximum(i - n_lat_tiles, 0), 0))]


def _token_tile(refs, n_lat_tiles):
    if len(refs) == 1:
        return refs[0][0]
    return jnp.where(pl.program_id(1) >= n_lat_tiles, refs[1][0], refs[0][0])


def _rotary(x, cos, sin_signed):
    width = x.shape[-1]
    lane = lax.broadcasted_iota(jnp.int32, x.shape, 1)
    partner = jnp.where((lane & 16) == 0, pltpu.roll(x, width - 16, 1), pltpu.roll(x, 16, 1))
    return x * cos + partner * sin_signed


def _inproj_kernel(n_lat_tiles, batch, n_x, *refs):
    x_refs = refs[:n_x]
    (mod_ref, gain_ref, w_ref, cos_ref, sin_ref, abp_ref,
     uf_ref, qkv_ref, z_ref, q_ref, k_ref, v_ref, gate_ref, gb_ref) = refs[n_x:]
    mod = _mod_row(mod_ref, n_lat_tiles, batch)
    sh = mod[:, 0:D_MODEL]
    sc = mod[:, D_MODEL:2 * D_MODEL]
    h = (_rms(_token_tile(x_refs, n_lat_tiles)) * gain_ref[...]) * (1.0 + sc) + sh
    hb = h.astype(BF16)

    def proj(c0, width):
        return _dot(hb, w_ref[:, c0:c0 + width])

    uf_ref[0] = proj(C_UF, F_WIDTH).astype(BF16)
    for j in range(3):
        qkv_ref[0, :, j * DN_WIDTH:(j + 1) * DN_WIDTH] = proj(C_QKV + j * DN_WIDTH, DN_WIDTH).astype(BF16)
    z_ref[0] = proj(C_Z, DN_WIDTH).astype(BF16)
    cos = jnp.tile(cos_ref[...], (1, DA_HEADS))
    sin = jnp.tile(sin_ref[...], (1, DA_HEADS))
    q = _rotary(proj(C_Q, DA_WIDTH), cos, sin)
    q_ref[0] = (q * (DA_QK_DIM ** -0.5 * LOG2_E)).astype(BF16)
    k_ref[0] = _rotary(proj(C_K, DA_WIDTH), cos, sin).astype(BF16)
    v_ref[0] = proj(C_V, DA_WIDTH).astype(BF16)
    for j in range(3 * D_MODEL // 512):
        g = proj(C_GATE + j * 512, 512)
        gate_ref[0, :, j * 512:(j + 1) * 512] = jax.nn.sigmoid(g).astype(BF16)
    ab = proj(C_AB, 256)
    abp = abp_ref[...]
    lane = lax.broadcasted_iota(jnp.int32, ab.shape, 1) % 128
    beta = jax.nn.sigmoid(ab)
    g = -jnp.exp(abp[0:1, :]) * _softplus(ab + abp[1:2, :])
    gb_ref[0] = jnp.where(lane < DN_HEADS, beta, g)


def input_projection(xs, mod, gain, w_big, cos_t, sin_t, abp, n_lat_tiles):
    batch, _, d = xs[0].shape
    t_all = sum(a.shape[1] for a in xs)
    n_tiles = t_all // ROW_TILE
    tm = ROW_TILE

    def rows(width, dtype):
        return (pl.BlockSpec((1, tm, width), lambda b, i: (b, i, 0)),
                jax.ShapeDtypeStruct((batch, t_all, width), dtype))

    outs = [rows(F_WIDTH, BF16), rows(3 * DN_WIDTH, BF16), rows(DN_WIDTH, BF16), rows(DA_WIDTH, BF16),
            rows(DA_WIDTH, BF16), rows(DA_WIDTH, BF16), rows(3 * D_MODEL, BF16), rows(256, F32)]
    return pl.pallas_call(
        functools.partial(_inproj_kernel, n_lat_tiles, batch, len(xs)),
        grid=(batch, n_tiles),
        in_specs=_token_specs(xs, n_lat_tiles, d) + [
            _resident(mod.shape), _resident(gain.shape), _resident(w_big.shape),
            pl.BlockSpec((tm, 128), lambda b, i: (i, 0)),
            pl.BlockSpec((tm, 128), lambda b, i: (i, 0)),
            _resident(abp.shape)],
        out_specs=[o[0] for o in outs],
        out_shape=[o[1] for o in outs],
        compiler_params=_params("parallel", "parallel"),
        name="input_projection",
    )(*xs, mod, gain, w_big, cos_t, sin_t, abp)


def _dft_tables(n_pos):
    n1 = n_pos // GRID_W
    n2 = GRID_W
    a = np.arange(n1)
    ang1 = 2.0 * np.pi * ((a[:, None] * a[None, :]) % n1) / n1
    stage1 = np.concatenate([np.cos(ang1), -np.sin(ang1)], axis=0)
    k1 = np.arange(n1)[:, None, None]
    k2 = np.arange(n2)[None, :, None]
    m = np.arange(n2)[None, None, :]
    ang2 = 2.0 * np.pi * ((m * (k1 + n1 * k2)) % n_pos) / n_pos
    c2, s2 = np.cos(ang2), np.sin(ang2)
    stage2 = np.concatenate([np.concatenate([c2, s2], axis=2),
                             np.concatenate([-s2, c2], axis=2)], axis=1)
    return stage1, stage2


def _channel_table(n_pos):
    c = np.arange(F_GROUP_DIM)
    ang = 2.0 * np.pi * ((c[:, None] * c[None, :]) % F_GROUP_DIM) / F_GROUP_DIM
    scale = 1.0 / math.sqrt(n_pos * F_GROUP_DIM)
    return np.concatenate([np.cos(ang), np.sin(ang)], axis=0) * scale


def _channel_mix(pr, pi, chan):
    outs = []
    for g in range(F_GROUPS):
        sl = slice(g * F_GROUP_DIM, (g + 1) * F_GROUP_DIM)
        pg = jnp.concatenate([pr[:, sl], pi[:, sl]], axis=1).astype(BF16)
        outs.append(_dot(pg, chan))
    return jnp.concatenate(outs, axis=1)


def _fft1_kernel(n1, f_ref, x_ref, ar_ref, ai_ref):
    a = _dot(f_ref[...].astype(BF16), x_ref[0])
    ar_ref[0] = a[:n1].astype(BF16)
    ai_ref[0] = a[n1:].astype(BF16)


def _fft2_kernel(kb, m_ref, ar_ref, ai_ref, chan_ref, o_ref):
    chan = chan_ref[...].astype(BF16)
    for j in range(kb):
        a = jnp.concatenate([ar_ref[0, j], ai_ref[0, j]], axis=0)
        p = _dot(m_ref[j].astype(BF16), a)
        y = _channel_mix(p[:GRID_W], p[GRID_W:], chan)
        o_ref[0, :, j * F_WIDTH:(j + 1) * F_WIDTH] = y.astype(BF16)


def fourier_latent(uf_all, n_pos):
    batch, t_all, _ = uf_all.shape
    n1 = n_pos // GRID_W
    stage1, stage2 = _dft_tables(n_pos)
    f1 = jnp.asarray(stage1, F32)
    m2 = jnp.asarray(stage2, F32)
    chan = jnp.asarray(_channel_table(n_pos), F32)
    wide = GRID_W * F_WIDTH
    x2 = uf_all.reshape(batch, t_all // GRID_W, wide)
    tn = 4096
    ar, ai = pl.pallas_call(
        functools.partial(_fft1_kernel, n1),
        grid=(batch, wide // tn),
        in_specs=[_resident(f1.shape),
                  pl.BlockSpec((1, n1, tn), lambda b, j: (b, 0, j))],
        out_specs=[pl.BlockSpec((1, n1, tn), lambda b, j: (b, 0, j))] * 2,
        out_shape=[jax.ShapeDtypeStruct((batch, n1, wide), BF16)] * 2,
        compiler_params=_params("parallel", "parallel"),
        name="fourier_stage1",
    )(f1, x2)
    ar = ar.reshape(batch, n1, GRID_W, F_WIDTH)
    ai = ai.reshape(batch, n1, GRID_W, F_WIDTH)
    kb = 8
    y = pl.pallas_call(
        functools.partial(_fft2_kernel, kb),
        grid=(batch, n1 // kb),
        in_specs=[pl.BlockSpec((kb, 2 * GRID_W, 2 * GRID_W), lambda b, i: (i, 0, 0)),
                  pl.BlockSpec((1, kb, GRID_W, F_WIDTH), lambda b, i: (b, i, 0, 0)),
                  pl.BlockSpec((1, kb, GRID_W, F_WIDTH), lambda b, i: (b, i, 0, 0)),
                  _resident(chan.shape)],
        out_specs=pl.BlockSpec((1, GRID_W, kb * F_WIDTH), lambda b, i: (b, 0, i)),
        out_shape=jax.ShapeDtypeStruct((batch, GRID_W, n1 * F_WIDTH), BF16),
        compiler_params=_params("parallel", "parallel"),
        name="fourier_stage2",
    )(m2, ar, ai, chan)
    return y.reshape(batch, n_pos, F_WIDTH)


def _fft_ctx_kernel(tc, f_ref, x_ref, chan_ref, o_ref):
    p = _dot(f_ref[...].astype(BF16), x_ref[0])
    o_ref[0] = _channel_mix(p[:tc], p[tc:], chan_ref[...].astype(BF16)).astype(BF16)


def fourier_context(uf_all, n_pos, tc):
    batch = uf_all.shape[0]
    a = np.arange(tc)
    ang = 2.0 * np.pi * ((a[:, None] * a[None, :]) % tc) / tc
    f = jnp.asarray(np.concatenate([np.cos(ang), -np.sin(ang)], axis=0), F32)
    chan = jnp.asarray(_channel_table(tc), F32)
    return pl.pallas_call(
        functools.partial(_fft_ctx_kernel, tc),
        grid=(batch,),
        in_specs=[_resident(f.shape),
                  pl.BlockSpec((1, tc, F_WIDTH), lambda b: (b, n_pos // tc, 0)),
                  _resident(chan.shape)],
        out_specs=pl.BlockSpec((1, tc, F_WIDTH), lambda b: (b, 0, 0)),
        out_shape=jax.ShapeDtypeStruct((batch, tc, F_WIDTH), BF16),
        compiler_params=_params("parallel"),
        name="fourier_context",
    )(f, uf_all, chan)


def _dnprep_kernel(n_lat_tiles, n_tiles, cur_ref, prev_ref, next_ref, w_ref, q_ref, k_ref, v_ref, ext_ref):
    i = pl.program_id(1)
    tm = ROW_TILE
    h = CONV_HALO
    prev_ok = jnp.logical_and(i != 0, i != n_lat_tiles)
    next_ok = jnp.logical_and(i != n_lat_tiles - 1, i != n_tiles - 1)
    ext_ref[0:h, :] = jnp.where(prev_ok, prev_ref[0].astype(F32), 0.0)
    ext_ref[h:h + tm, :] = cur_ref[0].astype(F32)
    ext_ref[h + tm:h + tm + h, :] = jnp.where(next_ok, next_ref[0].astype(F32), 0.0)
    pad = DN_CONV // 2
    acc = None
    for j in range(DN_CONV):
        term = ext_ref[h - pad + j:h - pad + j + tm, :] * w_ref[j:j + 1, :]
        acc = term if acc is None else acc + term
    u = _silu(acc)

    def l2n(block):
        return block * lax.rsqrt(jnp.sum(block * block, axis=-1, keepdims=True) + NORM_EPS)

    for hd in range(DN_HEADS):
        sl = slice(hd * DN_HEAD_DIM, (hd + 1) * DN_HEAD_DIM)
        q = u[:, hd * DN_HEAD_DIM:(hd + 1) * DN_HEAD_DIM]
        k = u[:, DN_WIDTH + hd * DN_HEAD_DIM:DN_WIDTH + (hd + 1) * DN_HEAD_DIM]
        q_ref[0, :, sl] = (l2n(q) * (DN_HEAD_DIM ** -0.5)).astype(BF16)
        k_ref[0, :, sl] = l2n(k).astype(BF16)
    v_ref[0] = u[:, 2 * DN_WIDTH:].astype(BF16)


def deltanet_prep(qkv, conv_w8, n_lat_tiles):
    batch, t_all, width = qkv.shape
    tm = ROW_TILE
    n_tiles = t_all // tm
    per = tm // CONV_HALO
    last = t_all // CONV_HALO - 1
    out = (pl.BlockSpec((1, tm, DN_WIDTH), lambda b, i: (b, i, 0)),
           jax.ShapeDtypeStruct((batch, t_all, DN_WIDTH), BF16))
    return pl.pallas_call(
        functools.partial(_dnprep_kernel, n_lat_tiles, n_tiles),
        grid=(batch, n_tiles),
        in_specs=[pl.BlockSpec((1, tm, width), lambda b, i: (b, i, 0)),
                  pl.BlockSpec((1, CONV_HALO, width), lambda b, i: (b, jnp.maximum(i * per - 1, 0), 0)),
                  pl.BlockSpec((1, CONV_HALO, width), lambda b, i: (b, jnp.minimum((i + 1) * per, last), 0)),
                  _resident(conv_w8.shape)],
        out_specs=[out[0]] * 3,
        out_shape=[out[1]] * 3,
        scratch_shapes=[pltpu.VMEM((tm + 2 * CONV_HALO, width), F32)],
        compiler_params=_params("parallel", "parallel"),
        name="deltanet_prep",
    )(qkv, qkv, qkv, conv_w8)


def _dnscan_kernel(batch, qf_ref, kf_ref, vf_ref, gf_ref, qb_ref, kb_ref, vb_ref, gb_ref, of_ref, ob_ref, s_ref):
    c = DN_CHUNK
    hh = DN_HEADS
    chains = [(d, b, h) for d in range(2) for b in range(batch) for h in range(hh)]
    n = len(chains)
    half = n // 2

    @pl.when(pl.program_id(0) == 0)
    def _():
        s_ref[...] = jnp.zeros_like(s_ref)

    i2 = lax.broadcasted_iota(jnp.int32, (c, c), 0)
    j2 = lax.broadcasted_iota(jnp.int32, (c, c), 1)
    tri = [(j2 <= i2).astype(F32), (j2 >= i2).astype(F32)]
    g_refs = (gf_ref, gb_ref)
    gcum, grow, gtot, graw = {}, {}, {}, {}
    for d in range(2):
        for b in range(batch):
            g = g_refs[d][b]
            graw[d, b] = g
            gcum[d, b] = _dot_f32(tri[d], g)
            grow[d, b] = lax.dot_general(g.T, tri[d], (((1,), (1,)), ((), ())), preferred_element_type=F32,
                                         precision=lax.Precision.HIGHEST)
            gtot[d, b] = jnp.sum(g, axis=0, keepdims=True)

    def per_chain(fn):
        return jnp.stack([fn(d, b, h) for d, b, h in chains])

    def tile(refs):
        return per_chain(lambda d, b, h: refs[d][b, :, h * DN_HEAD_DIM:(h + 1) * DN_HEAD_DIM]).astype(F32)

    q = tile((qf_ref, qb_ref))
    k = tile((kf_ref, kb_ref))
    v = tile((vf_ref, vb_ref))
    beta = per_chain(lambda d, b, h: graw[d, b][:, h:h + 1])
    gc = per_chain(lambda d, b, h: gcum[d, b][:, hh + h:hh + h + 1])
    gr = per_chain(lambda d, b, h: grow[d, b][hh + h:hh + h + 1, :])
    gl = per_chain(lambda d, b, h: gtot[d, b][:, hh + h:hh + h + 1])

    def bmm(a, bm, fn=_dot):
        return jnp.stack([fn(a[m], bm[m]) for m in range(n)])

    ch = lax.broadcasted_iota(jnp.int32, (n, c, c), 0)
    ii = lax.broadcasted_iota(jnp.int32, (n, c, c), 1)
    jj = lax.broadcasted_iota(jnp.int32, (n, c, c), 2)
    diff = jnp.where(ch < half, ii - jj, jj - ii)
    incl = diff >= 0
    strict = diff > 0
    eye = (ii == jj).astype(F32)

    decay = jnp.exp(jnp.where(incl, gc - gr, -jnp.inf))
    kbeta = k * beta
    k16 = k.astype(BF16)
    lmat = jnp.where(strict, bmm(kbeta.astype(BF16), k16, _dot_nt) * decay, 0.0)
    blk = lambda s: (ii // s) == (jj // s)
    t = eye - jnp.where(blk(2), lmat, 0.0)
    s = 2
    while s < c:
        ls = jnp.where(jnp.logical_and(blk(2 * s), jnp.logical_not(blk(s))), lmat, 0.0)
        t16 = t.astype(BF16)
        t = t - bmm(bmm(t16, ls.astype(BF16)).astype(BF16), t16)
        s *= 2
    eg = jnp.exp(gc)
    rhs = jnp.concatenate([v * beta, kbeta * eg], axis=2).astype(BF16)
    uw = bmm(t.astype(BF16), rhs)
    u = uw[:, :, :DN_HEAD_DIM]
    w = uw[:, :, DN_HEAD_DIM:]
    state = s_ref[...]
    s16 = state.astype(BF16)
    ws = bmm(jnp.concatenate([w, q * eg], axis=1).astype(BF16), s16)
    v_new = u - ws[:, :c]
    qk = bmm(q.astype(BF16), k16, _dot_nt) * decay
    v16 = v_new.astype(BF16)
    out = ws[:, c:] + bmm(qk.astype(BF16), v16)
    ke = (k * jnp.exp(gl - gc)).astype(BF16)
    s_ref[...] = state * jnp.exp(gl) + bmm(ke, v16, _dot_tn)
    o_refs = (of_ref, ob_ref)
    for m, (d, b, h) in enumerate(chains):
        o_refs[d][b, :, h * DN_HEAD_DIM:(h + 1) * DN_HEAD_DIM] = out[m].astype(BF16)


def deltanet_scan(qn, kn, vn, gb, n_lat):
    batch, t_all, _ = qn.shape
    c = DN_CHUNK
    n_chunks = t_all // c
    n_lat_chunks = n_lat // c
    n_ctx_chunks = n_chunks - n_lat_chunks

    def fwd(s):
        return jnp.where(s < n_ctx_chunks, n_lat_chunks + s, s - n_ctx_chunks)

    def bwd(s):
        return n_chunks - 1 - s

    tok_f = pl.BlockSpec((batch, c, DN_WIDTH), lambda s: (0, fwd(s), 0))
    tok_b = pl.BlockSpec((batch, c, DN_WIDTH), lambda s: (0, bwd(s), 0))
    out = jax.ShapeDtypeStruct((batch, t_all, DN_WIDTH), BF16)
    return pl.pallas_call(
        functools.partial(_dnscan_kernel, batch),
        grid=(n_chunks,),
        in_specs=[tok_f, tok_f, tok_f, pl.BlockSpec((batch, c, 128), lambda s: (0, fwd(s), 0)),
                  tok_b, tok_b, tok_b, pl.BlockSpec((batch, c, 128), lambda s: (0, bwd(s), 1))],
        out_specs=[tok_f, tok_b],
        out_shape=[out, out],
        scratch_shapes=[pltpu.VMEM((2 * batch * DN_HEADS, DN_HEAD_DIM, DN_HEAD_DIM), F32)],
        compiler_params=_params("arbitrary"),
        name="deltanet_scan",
    )(qn, kn, vn, gb, qn, kn, vn, gb)


def _attn_kernel(tq, tk, nk, lam_init, q_ref, k_ref, v_ref, lv_ref, gain_ref, o_ref,
                 qs_ref, vp_ref, s0_ref, s1_ref, m_ref, acc_ref):
    dv = DA_V_DIM

    @pl.when(pl.program_id(2) == 0)
    def _():
        vp_ref[:, 0:dv] = v_ref[0]
        vp_ref[:, dv:2 * dv] = jnp.ones((vp_ref.shape[0], dv), BF16)

    q = q_ref[0]
    lane = lax.broadcasted_iota(jnp.int32, q.shape, 1)
    zero = jnp.zeros_like(q)
    qs_ref[0:tq, :] = jnp.where(lane < DA_QK_DIM, q, zero)
    qs_ref[tq:2 * tq, :] = jnp.where(lane >= DA_QK_DIM, q, zero)
    m_ref[...] = jnp.full_like(m_ref, -jnp.inf)
    acc_ref[...] = jnp.zeros_like(acc_ref)
    s_refs = (s0_ref, s1_ref)
    halves = (slice(0, tq), slice(tq, 2 * tq))
    for half in halves:
        s0_ref[half, :] = _dot_nt(qs_ref[half, :], k_ref[0, 0:tk, :])

    def step(j, cur, prefetch):
        if prefetch:
            k_next = k_ref[0, pl.ds(pl.multiple_of((j + 1) * tk, tk), tk), :]
            s_refs[1 - cur][...] = _dot_nt(qs_ref[...], k_next)
        s = s_refs[cur][...]
        m_old = m_ref[...]
        m_new = jnp.maximum(m_old, jnp.max(s, axis=-1, keepdims=True))
        alpha = jnp.exp2(m_old - m_new)
        p = jnp.exp2(s - m_new).astype(BF16)
        vp = vp_ref[pl.ds(pl.multiple_of(j * tk, tk), tk), :]
        if prefetch:
            acc_ref[...] = alpha * acc_ref[...] + _dot(p, vp)
        else:
            for half in halves:
                acc_ref[half, :] = alpha[half] * acc_ref[half, :] + _dot(p[half], vp)
        m_ref[...] = m_new

    n_pairs = (nk - 1) // 2
    if n_pairs > 0:
        def body(i, carry):
            step(2 * i, 0, True)
            step(2 * i + 1, 1, True)
            return carry
        lax.fori_loop(0, n_pairs, body, 0)
    for j in range(2 * n_pairs, nk):
        step(j, j % 2, j + 1 < nk)

    lv = lv_ref[...]
    lam = (jnp.exp(jnp.sum(lv[0:1] * lv[1:2], axis=-1, keepdims=True))
           - jnp.exp(jnp.sum(lv[2:3] * lv[3:4], axis=-1, keepdims=True)) + lam_init)
    acc = acc_ref[...]
    o = acc[:, :dv] / acc[:, dv:]
    o = o[:tq] - lam * o[tq:]
    o_ref[0] = (_rms(o) * gain_ref[...] * (1.0 - lam_init)).astype(BF16)


def diff_attention(q, k, v, lam_vecs, gain, lam_init, q_rows, q_off, k_rows, k_off, tq, tk):
    batch = q.shape[0]
    nq, nk = q_rows // tq, k_rows // tk
    qo, ko = q_off // tq, k_off // k_rows
    kv = pl.BlockSpec((1, k_rows, DA_V_DIM), lambda b, h, i: (b, ko, h))
    return pl.pallas_call(
        functools.partial(_attn_kernel, tq, tk, nk, lam_init),
        grid=(batch, DA_HEADS, nq),
        in_specs=[pl.BlockSpec((1, tq, DA_V_DIM), lambda b, h, i: (b, qo + i, h)), kv, kv,
                  pl.BlockSpec(lam_vecs.shape, lambda b, h, i: (0, 0)),
                  pl.BlockSpec(gain.shape, lambda b, h, i: (0, 0))],
        out_specs=pl.BlockSpec((1, tq, DA_V_DIM), lambda b, h, i: (b, i, h)),
        out_shape=jax.ShapeDtypeStruct((batch, q_rows, DA_WIDTH), BF16),
        scratch_shapes=[pltpu.VMEM((2 * tq, DA_V_DIM), BF16), pltpu.VMEM((k_rows, 2 * DA_V_DIM), BF16),
                        pltpu.VMEM((2 * tq, tk), F32), pltpu.VMEM((2 * tq, tk), F32), pltpu.VMEM((2 * tq, 1), F32),
                        pltpu.VMEM((2 * tq, 2 * DA_V_DIM), F32)],
        compiler_params=_params("parallel", "parallel", "arbitrary"),
        name="diff_attention",
    )(q, k, v, lam_vecs, gain)


def _merge_mlp_kernel(n_lat_tiles, batch, n_x, has_ctx, final, *refs):
    x_refs = refs[:n_x]
    refs = refs[n_x:]
    if has_ctx:
        (mod_ref, yf_ref, yfc_ref, odnf_ref, odnb_ref, z_ref, oda_ref, odac_ref, gate_ref, dng_ref,
         wf_ref, wdn_ref, wda_ref, wo_ref, gain2_ref, w1_ref, w2_ref, fg_ref, o_ref) = refs
    else:
        (mod_ref, yf_ref, odnf_ref, odnb_ref, z_ref, oda_ref, gate_ref, dng_ref,
         wf_ref, wdn_ref, wda_ref, wo_ref, gain2_ref, w1_ref, w2_ref, fg_ref, o_ref) = refs
    mod = _mod_row(mod_ref, n_lat_tiles, batch)
    g1 = mod[:, 2 * D_MODEL:3 * D_MODEL]
    yf = yf_ref[0]
    oda = oda_ref[0]
    if has_ctx:
        is_ctx = pl.program_id(1) >= n_lat_tiles
        yf = jnp.where(is_ctx, yfc_ref[0], yf)
        oda = jnp.where(is_ctx, odac_ref[0], oda)
    o = odnf_ref[0].astype(F32) + odnb_ref[0].astype(F32)
    z = z_ref[0].astype(F32)
    dn_parts = []
    for hd in range(DN_HEADS):
        sl = slice(hd * DN_HEAD_DIM, (hd + 1) * DN_HEAD_DIM)
        dn_parts.append(_rms(o[:, sl]) * dng_ref[...] * _silu(z[:, sl]))
    odn = jnp.concatenate(dn_parts, axis=1).astype(BF16)
    gate = gate_ref[0].astype(F32)
    merged = (gate[:, 0:D_MODEL] * _dot(yf, wf_ref[...])
              + gate[:, D_MODEL:2 * D_MODEL] * _dot(odn, wdn_ref[...])
              + gate[:, 2 * D_MODEL:] * _dot(oda, wda_ref[...]))
    y = _dot(merged.astype(BF16), wo_ref[...])
    x = _token_tile(x_refs, n_lat_tiles) + g1 * y
    sh = mod[:, 3 * D_MODEL:4 * D_MODEL]
    sc = mod[:, 4 * D_MODEL:5 * D_MODEL]
    g2 = mod[:, 5 * D_MODEL:]
    h = ((_rms(x) * gain2_ref[...]) * (1.0 + sc) + sh).astype(BF16)
    a = jnp.maximum(_dot(h, w1_ref[...]), 0.0)
    out = x + g2 * _dot((a * a).astype(BF16), w2_ref[...])
    if final:
        out = _rms(out) * fg_ref[...]
    o_ref[0] = out


def merge_and_mlp(xs, mod, yf, yf_ctx, odn, z, oda, oda_ctx, gates, dn_gain, w_f, w_dn, w_da, w_o,
                  gain2, w1, w2, final_gain, n_lat_tiles, n_tiles, final):
    batch, _, d = xs[0].shape
    tm = ROW_TILE
    has_ctx = yf_ctx is not None
    lat_last = n_lat_tiles - 1

    def rows(width):
        return pl.BlockSpec((1, tm, width), lambda b, i: (b, i, 0))

    def lat_rows(width):
        return pl.BlockSpec((1, tm, width), lambda b, i: (b, jnp.minimum(i, lat_last), 0))

    def ctx_rows(width):
        return pl.BlockSpec((1, tm, width), lambda b, i: (b, jnp.maximum(i - n_lat_tiles, 0), 0))

    args = list(xs) + [mod, yf]
    specs = _token_specs(xs, n_lat_tiles, d) + [_resident(mod.shape), lat_rows(F_WIDTH)]
    if has_ctx:
        args.append(yf_ctx)
        specs.append(ctx_rows(F_WIDTH))
    args += [odn[0], odn[1], z, oda]
    specs += [rows(DN_WIDTH), rows(DN_WIDTH), rows(DN_WIDTH), lat_rows(DA_WIDTH)]
    if has_ctx:
        args.append(oda_ctx)
        specs.append(ctx_rows(DA_WIDTH))
    weights = [dn_gain, w_f, w_dn, w_da, w_o, gain2, w1, w2, final_gain]
    args += [gates] + weights
    specs += [rows(3 * D_MODEL)] + [_resident(w.shape) for w in weights]
    return pl.pallas_call(
        functools.partial(_merge_mlp_kernel, n_lat_tiles, batch, len(xs), has_ctx, final),
        grid=(batch, n_tiles),
        in_specs=specs,
        out_specs=rows(d),
        out_shape=jax.ShapeDtypeStruct((batch, n_tiles * tm, d), F32),
        compiler_params=_params("parallel", "parallel"),
        name="merge_and_mlp",
    )(*args)


def _arrange_w_in(w_in):
    splits = np.cumsum([F_WIDTH, 3 * DN_WIDTH, DN_WIDTH, 4 * DN_HEADS, DA_WIDTH, DA_WIDTH, DA_WIDTH])
    w_uf, w_qkv, w_z, w_ab, w_q, w_k, w_v, w_g = jnp.split(w_in, splits, axis=1)
    h = DN_HEADS
    pad = jnp.zeros((w_in.shape[0], 128 - 2 * h), w_in.dtype)
    ab = jnp.concatenate([w_ab[:, 0:h], w_ab[:, 2 * h:3 * h], pad,
                          w_ab[:, h:2 * h], w_ab[:, 3 * h:4 * h], pad], axis=1)
    cols = [w_uf, w_qkv, w_z, w_q, w_k, w_v, w_g, ab]
    return jnp.concatenate(cols, axis=1).astype(BF16)


def _rope_tables(n_lat, n_ctx):
    t = jnp.arange(n_lat)
    row = (t // GRID_W).astype(F32)
    col = (t % GRID_W).astype(F32)
    n_freq = DA_QK_DIM // 4
    inv_freq = ROPE_THETA ** (-jnp.arange(n_freq, dtype=F32) / n_freq)
    ang_r = row[:, None] * inv_freq[None, :]
    ang_c = col[:, None] * inv_freq[None, :]
    ang = jnp.concatenate([ang_r, ang_r, ang_c, ang_c], axis=1)
    sign = jnp.asarray(np.where((np.arange(DA_QK_DIM) & 16) == 0, -1.0, 1.0), F32)
    cos = jnp.concatenate([jnp.tile(jnp.cos(ang), (1, 2)), jnp.ones((n_ctx, 128), F32)], axis=0)
    sin = jnp.concatenate([jnp.tile(jnp.sin(ang) * sign, (1, 2)), jnp.zeros((n_ctx, 128), F32)], axis=0)
    return cos, sin


def _decay_params(a_log, dt_bias):
    h = DN_HEADS
    out = jnp.zeros((8, 256), F32)
    for d in range(2):
        out = out.at[0, d * 128 + h:d * 128 + 2 * h].set(a_log[d])
        out = out.at[1, d * 128 + h:d * 128 + 2 * h].set(dt_bias[d])
    return out


def kernel(x, c, ctx, c_ctx, norm1, norm2, w_ada, b_ada, w_in, conv_w, a_log, dt_bias, dn_gain, lam_vecs,
           da_gain, w_f, w_dn, w_da, w_o, w_mlp1, w_mlp2, final_norm):
    batch, n_lat, d = x.shape
    n_ctx = ctx.shape[1]
    depth = w_in.shape[0]
    t_all = n_lat + n_ctx
    n_lat_tiles = n_lat // ROW_TILE
    n_tiles = t_all // ROW_TILE

    cvec = jnp.zeros((8, d), F32).at[:batch].set(c).at[batch].set(c_ctx)
    mods = ada_modulation(cvec, w_ada, b_ada)
    cos_t, sin_t = _rope_tables(n_lat, n_ctx)
    xs = (x, ctx)
    final_gain = final_norm.reshape(1, d)

    for l in range(depth):
        last = l == depth - 1
        lam_init = 0.8 - 0.6 * math.exp(-0.3 * l)
        mod = mods[l]
        w_big = _arrange_w_in(w_in[l])
        abp = _decay_params(a_log[l], dt_bias[l])
        uf, qkv, z, q, k, v, gates, gb = input_projection(
            xs, mod, norm1[l].reshape(1, d), w_big, cos_t, sin_t, abp, n_lat_tiles)

        yf = fourier_latent(uf, n_lat)
        conv_w8 = jnp.zeros((8, 3 * DN_WIDTH), F32).at[:DN_CONV].set(conv_w[l])
        qn, kn, vn = deltanet_prep(qkv, conv_w8, n_lat_tiles)
        odn = deltanet_scan(qn, kn, vn, gb, n_lat)
        gain_da = da_gain[l].reshape(1, DA_V_DIM)
        tk = next(t for t in (768, 512, 256) if t_all % t == 0)
        tq = min(ATTN_Q_TILE, n_lat)
        oda = diff_attention(q, k, v, lam_vecs[l], gain_da, lam_init, n_lat, 0, t_all, 0, tq, tk)
        if last:
            yf_ctx = oda_ctx = None
            tiles = n_lat_tiles
        else:
            yf_ctx = fourier_context(uf, n_lat, n_ctx)
            oda_ctx = diff_attention(q, k, v, lam_vecs[l], gain_da, lam_init, n_ctx, n_lat, n_ctx, n_lat,
                                     n_ctx, n_ctx)
            tiles = n_tiles
        out = merge_and_mlp(xs, mod, yf, yf_ctx, odn, z, oda, oda_ctx, gates, dn_gain[l].reshape(1, DN_HEAD_DIM),
                            w_f[l].astype(BF16), w_dn[l].astype(BF16), w_da[l].astype(BF16), w_o[l].astype(BF16),
                            norm2[l].reshape(1, d), w_mlp1[l].astype(BF16), w_mlp2[l].astype(BF16), final_gain,
                            n_lat_tiles, tiles, last)
        xs = (out,)
    return xs[0]
```

```python
import functools
import math

import numpy as np
import jax
import jax.numpy as jnp
from jax import lax
from jax.experimental import pallas as pl
from jax.experimental.pallas import tpu as pltpu

F32 = jnp.float32
BF16 = jnp.bfloat16

D_MODEL = 1024
GRID_W = 64
F_GROUPS = 4
F_GROUP_DIM = 128
F_WIDTH = F_GROUPS * F_GROUP_DIM
DN_HEADS = 4
DN_HEAD_DIM = 128
DN_WIDTH = DN_HEADS * DN_HEAD_DIM
DN_CONV = 5
DN_CHUNK = 64
DA_HEADS = 4
DA_QK_DIM = 64
DA_V_DIM = 2 * DA_QK_DIM
DA_WIDTH = DA_HEADS * DA_V_DIM
ROPE_THETA = 10000.0
D_FF = 4 * D_MODEL
NORM_EPS = 1e-6
LOG2_E = math.log2(math.e)

ROW_TILE = 256
ATTN_Q_TILE = 1024
CONV_HALO = 16
VMEM_LIMIT = 56 * 1024 * 1024

C_UF = 0
C_QKV = C_UF + F_WIDTH
C_Z = C_QKV + 3 * DN_WIDTH
C_Q = C_Z + DN_WIDTH
C_K = C_Q + DA_WIDTH
C_V = C_K + DA_WIDTH
C_GATE = C_V + DA_WIDTH
C_AB = C_GATE + 3 * D_MODEL
W_IN_COLS = C_AB + 256


def _dot(a, b):
    return jnp.dot(a, b, preferred_element_type=F32)


def _dot_f32(a, b):
    return jnp.dot(a, b, preferred_element_type=F32, precision=lax.Precision.HIGHEST)


def _dot_nt(a, b):
    return lax.dot_general(a, b, (((1,), (1,)), ((), ())), preferred_element_type=F32)


def _dot_tn(a, b):
    return lax.dot_general(a, b, (((0,), (0,)), ((), ())), preferred_element_type=F32)


def _silu(x):
    return x * jax.nn.sigmoid(x)


def _softplus(x):
    return jnp.maximum(x, 0.0) + jnp.log1p(jnp.exp(-jnp.abs(x)))


def _params(*sem):
    return pltpu.CompilerParams(dimension_semantics=sem, vmem_limit_bytes=VMEM_LIMIT)


def _resident(shape):
    nd = len(shape)
    return pl.BlockSpec(shape, lambda *_: (0,) * nd, pipeline_mode=pl.Buffered(1))


def _ada_kernel(c_ref, w_ref, b_ref, o_ref):
    a = _silu(c_ref[...])
    o_ref[0] = _dot_f32(a, w_ref[0]) + b_ref[0]


def ada_modulation(cvec, w_ada, b_ada):
    depth, d, n = w_ada.shape
    tn = 1536
    return pl.pallas_call(
        _ada_kernel,
        grid=(depth, n // tn),
        in_specs=[pl.BlockSpec((8, d), lambda l, j: (0, 0)),
                  pl.BlockSpec((1, d, tn), lambda l, j: (l, 0, j)),
                  pl.BlockSpec((1, 1, tn), lambda l, j: (l, 0, j))],
        out_specs=pl.BlockSpec((1, 8, tn), lambda l, j: (l, 0, j)),
        out_shape=jax.ShapeDtypeStruct((depth, 8, n), F32),
        compiler_params=_params("arbitrary", "arbitrary"),
        name="ada_modulation",
    )(cvec, w_ada, b_ada.reshape(depth, 1, n))


def _mod_row(mod_ref, n_lat_tiles, batch):
    i = pl.program_id(1)
    b = pl.program_id(0)
    r = jnp.where(i >= n_lat_tiles, batch, b)
    return mod_ref[pl.ds(r, 1), :]


def _rms(x):
    return x * lax.rsqrt(jnp.mean(x * x, axis=-1, keepdims=True) + NORM_EPS)


def _token_specs(arrays, n_lat_tiles, width):
    tm = ROW_TILE
    if len(arrays) == 1:
        return [pl.BlockSpec((1, tm, width), lambda b, i: (b, i, 0))]
    return [pl.BlockSpec((1, tm, width), lambda b, i: (b, jnp.minimum(i, n_lat_tiles - 1), 0)),
            pl.BlockSpec((1, tm, width), lambda b, i: (b, jnp.maximum(i - n_lat_tiles, 0), 0))]


def _token_tile(refs, n_lat_tiles):
    if len(refs) == 1:
        return refs[0][0]
    return jnp.where(pl.program_id(1) >= n_lat_tiles, refs[1][0], refs[0][0])


def _rotary(x, cos, sin_signed):
    width = x.shape[-1]
    lane = lax.broadcasted_iota(jnp.int32, x.shape, 1)
    partner = jnp.where((lane & 16) == 0, pltpu.roll(x, width - 16, 1), pltpu.roll(x, 16, 1))
    return x * cos + partner * sin_signed


def _inproj_kernel(n_lat_tiles, batch, n_x, *refs):
    x_refs = refs[:n_x]
    (mod_ref, gain_ref, w_ref, cos_ref, sin_ref, abp_ref,
     uf_ref, qkv_ref, z_ref, q_ref, k_ref, v_ref, gate_ref, gb_ref) = refs[n_x:]
    mod = _mod_row(mod_ref, n_lat_tiles, batch)
    sh = mod[:, 0:D_MODEL]
    sc = mod[:, D_MODEL:2 * D_MODEL]
    h = (_rms(_token_tile(x_refs, n_lat_tiles)) * gain_ref[...]) * (1.0 + sc) + sh
    hb = h.astype(BF16)

    def proj(c0, width):
        return _dot(hb, w_ref[:, c0:c0 + width])

    uf_ref[0] = proj(C_UF, F_WIDTH).astype(BF16)
    for j in range(3):
        qkv_ref[0, :, j * DN_WIDTH:(j + 1) * DN_WIDTH] = proj(C_QKV + j * DN_WIDTH, DN_WIDTH).astype(BF16)
    z_ref[0] = proj(C_Z, DN_WIDTH).astype(BF16)
    cos = jnp.tile(cos_ref[...], (1, DA_HEADS))
    sin = jnp.tile(sin_ref[...], (1, DA_HEADS))
    q = _rotary(proj(C_Q, DA_WIDTH), cos, sin)
    q_ref[0] = (q * (DA_QK_DIM ** -0.5 * LOG2_E)).astype(BF16)
    k_ref[0] = _rotary(proj(C_K, DA_WIDTH), cos, sin).astype(BF16)
    v_ref[0] = proj(C_V, DA_WIDTH).astype(BF16)
    for j in range(3 * D_MODEL // 512):
        g = proj(C_GATE + j * 512, 512)
        gate_ref[0, :, j * 512:(j + 1) * 512] = jax.nn.sigmoid(g).astype(BF16)
    ab = proj(C_AB, 256)
    abp = abp_ref[...]
    lane = lax.broadcasted_iota(jnp.int32, ab.shape, 1) % 128
    beta = jax.nn.sigmoid(ab)
    g = -jnp.exp(abp[0:1, :]) * _softplus(ab + abp[1:2, :])
    gb_ref[0] = jnp.where(lane < DN_HEADS, beta, g)


def input_projection(xs, mod, gain, w_big, cos_t, sin_t, abp, n_lat_tiles):
    batch, _, d = xs[0].shape
    t_all = sum(a.shape[1] for a in xs)
    n_tiles = t_all // ROW_TILE
    tm = ROW_TILE

    def rows(width, dtype):
        return (pl.BlockSpec((1, tm, width), lambda b, i: (b, i, 0)),
                jax.ShapeDtypeStruct((batch, t_all, width), dtype))

    outs = [rows(F_WIDTH, BF16), rows(3 * DN_WIDTH, BF16), rows(DN_WIDTH, BF16), rows(DA_WIDTH, BF16),
            rows(DA_WIDTH, BF16), rows(DA_WIDTH, BF16), rows(3 * D_MODEL, BF16), rows(256, F32)]
    return pl.pallas_call(
        functools.partial(_inproj_kernel, n_lat_tiles, batch, len(xs)),
        grid=(batch, n_tiles),
        in_specs=_token_specs(xs, n_lat_tiles, d) + [
            _resident(mod.shape), _resident(gain.shape), _resident(w_big.shape),
            pl.BlockSpec((tm, 128), lambda b, i: (i, 0)),
            pl.BlockSpec((tm, 128), lambda b, i: (i, 0)),
            _resident(abp.shape)],
        out_specs=[o[0] for o in outs],
        out_shape=[o[1] for o in outs],
        compiler_params=_params("parallel", "parallel"),
        name="input_projection",
    )(*xs, mod, gain, w_big, cos_t, sin_t, abp)


def _dft_tables(n_pos):
    n1 = n_pos // GRID_W
    n2 = GRID_W
    a = np.arange(n1)
    ang1 = 2.0 * np.pi * ((a[:, None] * a[None, :]) % n1) / n1
    stage1 = np.concatenate([np.cos(ang1), -np.sin(ang1)], axis=0)
    k1 = np.arange(n1)[:, None, None]
    k2 = np.arange(n2)[None, :, None]
    m = np.arange(n2)[None, None, :]
    ang2 = 2.0 * np.pi * ((m * (k1 + n1 * k2)) % n_pos) / n_pos
    c2, s2 = np.cos(ang2), np.sin(ang2)
    stage2 = np.concatenate([np.concatenate([c2, s2], axis=2),
                             np.concatenate([-s2, c2], axis=2)], axis=1)
    return stage1, stage2


def _channel_table(n_pos):
    c = np.arange(F_GROUP_DIM)
    ang = 2.0 * np.pi * ((c[:, None] * c[None, :]) % F_GROUP_DIM) / F_GROUP_DIM
    scale = 1.0 / math.sqrt(n_pos * F_GROUP_DIM)
    return np.concatenate([np.cos(ang), np.sin(ang)], axis=0) * scale


def _channel_mix(pr, pi, chan):
    outs = []
    for g in range(F_GROUPS):
        sl = slice(g * F_GROUP_DIM, (g + 1) * F_GROUP_DIM)
        pg = jnp.concatenate([pr[:, sl], pi[:, sl]], axis=1).astype(BF16)
        outs.append(_dot(pg, chan))
    return jnp.concatenate(outs, axis=1)


def _fft1_kernel(n1, f_ref, x_ref, ar_ref, ai_ref):
    a = _dot(f_ref[...].astype(BF16), x_ref[0])
    ar_ref[0] = a[:n1].astype(BF16)
    ai_ref[0] = a[n1:].astype(BF16)


def _fft2_kernel(kb, m_ref, ar_ref, ai_ref, chan_ref, o_ref):
    chan = chan_ref[...].astype(BF16)
    a = [jnp.concatenate([ar_ref[0, j], ai_ref[0, j]], axis=0) for j in range(kb)]
    p = [_dot(m_ref[j].astype(BF16), a[j]) for j in range(kb)]
    pg = [[jnp.concatenate([p[j][:GRID_W, g * F_GROUP_DIM:(g + 1) * F_GROUP_DIM],
                            p[j][GRID_W:, g * F_GROUP_DIM:(g + 1) * F_GROUP_DIM]], axis=1).astype(BF16)
           for g in range(F_GROUPS)] for j in range(kb)]
    y = [[_dot(pg[j][g], chan) for g in range(F_GROUPS)] for j in range(kb)]
    for j in range(kb):
        o_ref[0, :, j * F_WIDTH:(j + 1) * F_WIDTH] = jnp.concatenate(y[j], axis=1).astype(BF16)


def fourier_latent(uf_all, n_pos):
    batch, t_all, _ = uf_all.shape
    n1 = n_pos // GRID_W
    stage1, stage2 = _dft_tables(n_pos)
    f1 = jnp.asarray(stage1, F32)
    m2 = jnp.asarray(stage2, F32)
    chan = jnp.asarray(_channel_table(n_pos), F32)
    wide = GRID_W * F_WIDTH
    x2 = uf_all.reshape(batch, t_all // GRID_W, wide)
    tn = 4096
    ar, ai = pl.pallas_call(
        functools.partial(_fft1_kernel, n1),
        grid=(batch, wide // tn),
        in_specs=[_resident(f1.shape),
                  pl.BlockSpec((1, n1, tn), lambda b, j: (b, 0, j))],
        out_specs=[pl.BlockSpec((1, n1, tn), lambda b, j: (b, 0, j))] * 2,
        out_shape=[jax.ShapeDtypeStruct((batch, n1, wide), BF16)] * 2,
        compiler_params=_params("parallel", "parallel"),
        name="fourier_stage1",
    )(f1, x2)
    ar = ar.reshape(batch, n1, GRID_W, F_WIDTH)
    ai = ai.reshape(batch, n1, GRID_W, F_WIDTH)
    kb = 8
    y = pl.pallas_call(
        functools.partial(_fft2_kernel, kb),
        grid=(batch, n1 // kb),
        in_specs=[pl.BlockSpec((kb, 2 * GRID_W, 2 * GRID_W), lambda b, i: (i, 0, 0)),
                  pl.BlockSpec((1, kb, GRID_W, F_WIDTH), lambda b, i: (b, i, 0, 0)),
                  pl.BlockSpec((1, kb, GRID_W, F_WIDTH), lambda b, i: (b, i, 0, 0)),
                  _resident(chan.shape)],
        out_specs=pl.BlockSpec((1, GRID_W, kb * F_WIDTH), lambda b, i: (b, 0, i)),
        out_shape=jax.ShapeDtypeStruct((batch, GRID_W, n1 * F_WIDTH), BF16),
        compiler_params=_params("parallel", "parallel"),
        name="fourier_stage2",
    )(m2, ar, ai, chan)
    return y.reshape(batch, n_pos, F_WIDTH)


def _fft_ctx_kernel(tc, f_ref, x_ref, chan_ref, o_ref):
    p = _dot(f_ref[...].astype(BF16), x_ref[0])
    o_ref[0] = _channel_mix(p[:tc], p[tc:], chan_ref[...].astype(BF16)).astype(BF16)


def fourier_context(uf_all, n_pos, tc):
    batch = uf_all.shape[0]
    a = np.arange(tc)
    ang = 2.0 * np.pi * ((a[:, None] * a[None, :]) % tc) / tc
    f = jnp.asarray(np.concatenate([np.cos(ang), -np.sin(ang)], axis=0), F32)
    chan = jnp.asarray(_channel_table(tc), F32)
    return pl.pallas_call(
        functools.partial(_fft_ctx_kernel, tc),
        grid=(batch,),
        in_specs=[_resident(f.shape),
                  pl.BlockSpec((1, tc, F_WIDTH), lambda b: (b, n_pos // tc, 0)),
                  _resident(chan.shape)],
        out_specs=pl.BlockSpec((1, tc, F_WIDTH), lambda b: (b, 0, 0)),
        out_shape=jax.ShapeDtypeStruct((batch, tc, F_WIDTH), BF16),
        compiler_params=_params("parallel"),
        name="fourier_context",
    )(f, uf_all, chan)


def _dnprep_kernel(n_lat_tiles, n_tiles, cur_ref, prev_ref, next_ref, w_ref, q_ref, k_ref, v_ref, ext_ref):
    i = pl.program_id(1)
    tm = ROW_TILE
    h = CONV_HALO
    prev_ok = jnp.logical_and(i != 0, i != n_lat_tiles)
    next_ok = jnp.logical_and(i != n_lat_tiles - 1, i != n_tiles - 1)
    ext_ref[0:h, :] = jnp.where(prev_ok, prev_ref[0].astype(F32), 0.0)
    ext_ref[h:h + tm, :] = cur_ref[0].astype(F32)
    ext_ref[h + tm:h + tm + h, :] = jnp.where(next_ok, next_ref[0].astype(F32), 0.0)
    pad = DN_CONV // 2
    acc = None
    for j in range(DN_CONV):
        term = ext_ref[h - pad + j:h - pad + j + tm, :] * w_ref[j:j + 1, :]
        acc = term if acc is None else acc + term
    u = _silu(acc)

    def l2n(block):
        return block * lax.rsqrt(jnp.sum(block * block, axis=-1, keepdims=True) + NORM_EPS)

    for hd in range(DN_HEADS):
        sl = slice(hd * DN_HEAD_DIM, (hd + 1) * DN_HEAD_DIM)
        q = u[:, hd * DN_HEAD_DIM:(hd + 1) * DN_HEAD_DIM]
        k = u[:, DN_WIDTH + hd * DN_HEAD_DIM:DN_WIDTH + (hd + 1) * DN_HEAD_DIM]
        q_ref[0, :, sl] = (l2n(q) * (DN_HEAD_DIM ** -0.5)).astype(BF16)
        k_ref[0, :, sl] = l2n(k).astype(BF16)
    v_ref[0] = u[:, 2 * DN_WIDTH:].astype(BF16)


def deltanet_prep(qkv, conv_w8, n_lat_tiles):
    batch, t_all, width = qkv.shape
    tm = ROW_TILE
    n_tiles = t_all // tm
    per = tm // CONV_HALO
    last = t_all // CONV_HALO - 1
    out = (pl.BlockSpec((1, tm, DN_WIDTH), lambda b, i: (b, i, 0)),
           jax.ShapeDtypeStruct((batch, t_all, DN_WIDTH), BF16))
    return pl.pallas_call(
        functools.partial(_dnprep_kernel, n_lat_tiles, n_tiles),
        grid=(batch, n_tiles),
        in_specs=[pl.BlockSpec((1, tm, width), lambda b, i: (b, i, 0)),
                  pl.BlockSpec((1, CONV_HALO, width), lambda b, i: (b, jnp.maximum(i * per - 1, 0), 0)),
                  pl.BlockSpec((1, CONV_HALO, width), lambda b, i: (b, jnp.minimum((i + 1) * per, last), 0)),
                  _resident(conv_w8.shape)],
        out_specs=[out[0]] * 3,
        out_shape=[out[1]] * 3,
        scratch_shapes=[pltpu.VMEM((tm + 2 * CONV_HALO, width), F32)],
        compiler_params=_params("parallel", "parallel"),
        name="deltanet_prep",
    )(qkv, qkv, qkv, conv_w8)


def _dnscan_kernel(batch, qf_ref, kf_ref, vf_ref, gf_ref, qb_ref, kb_ref, vb_ref, gb_ref, of_ref, ob_ref, s_ref):
    c = DN_CHUNK
    hh = DN_HEADS
    chains = [(d, b, h) for d in range(2) for b in range(batch) for h in range(hh)]
    n = len(chains)
    half = n // 2

    @pl.when(pl.program_id(0) == 0)
    def _():
        s_ref[...] = jnp.zeros_like(s_ref)

    row = lax.broadcasted_iota(jnp.int32, (c, 128), 0)

    def prefix_sum(g, d):
        x = g
        shift = 1
        while shift < c:
            if d == 0:
                x = x + jnp.where(row >= shift, pltpu.roll(x, shift, 0), 0.0)
            else:
                x = x + jnp.where(row < c - shift, pltpu.roll(x, c - shift, 0), 0.0)
            shift *= 2
        return x

    g_refs = (gf_ref, gb_ref)
    gcum, grow, gtot, graw = {}, {}, {}, {}
    for d in range(2):
        for b in range(batch):
            g = g_refs[d][b]
            graw[d, b] = g
            gcum[d, b] = prefix_sum(g, d)
            grow[d, b] = gcum[d, b].T
            gtot[d, b] = jnp.sum(g, axis=0, keepdims=True)

    def per_chain(fn):
        return jnp.stack([fn(d, b, h) for d, b, h in chains])

    def tile(refs):
        return per_chain(lambda d, b, h: refs[d][b, :, h * DN_HEAD_DIM:(h + 1) * DN_HEAD_DIM]).astype(F32)

    q = tile((qf_ref, qb_ref))
    k = tile((kf_ref, kb_ref))
    v = tile((vf_ref, vb_ref))
    beta = per_chain(lambda d, b, h: graw[d, b][:, h:h + 1])
    gc = per_chain(lambda d, b, h: gcum[d, b][:, hh + h:hh + h + 1])
    gr = per_chain(lambda d, b, h: grow[d, b][hh + h:hh + h + 1, :])
    gl = per_chain(lambda d, b, h: gtot[d, b][:, hh + h:hh + h + 1])

    def bmm(a, bm, fn=_dot):
        return jnp.stack([fn(a[m], bm[m]) for m in range(n)])

    ch = lax.broadcasted_iota(jnp.int32, (n, c, c), 0)
    ii = lax.broadcasted_iota(jnp.int32, (n, c, c), 1)
    jj = lax.broadcasted_iota(jnp.int32, (n, c, c), 2)
    diff = jnp.where(ch < half, ii - jj, jj - ii)
    incl = diff >= 0
    strict = diff > 0
    eye = (ii == jj).astype(F32)

    decay = jnp.exp(jnp.where(incl, gc - gr, -jnp.inf))
    kbeta = k * beta
    k16 = k.astype(BF16)
    lmat = jnp.where(strict, bmm(kbeta.astype(BF16), k16, _dot_nt) * decay, 0.0)
    blk = lambda s: (ii // s) == (jj // s)
    t = eye - jnp.where(blk(2), lmat, 0.0)
    s = 2
    while s < c:
        ls = jnp.where(jnp.logical_and(blk(2 * s), jnp.logical_not(blk(s))), lmat, 0.0)
        t16 = t.astype(BF16)
        t = t - bmm(bmm(t16, ls.astype(BF16)).astype(BF16), t16)
        s *= 2
    eg = jnp.exp(gc)
    rhs = jnp.concatenate([v * beta, kbeta * eg], axis=2).astype(BF16)
    uw = bmm(t.astype(BF16), rhs)
    u = uw[:, :, :DN_HEAD_DIM]
    w = uw[:, :, DN_HEAD_DIM:]
    state = s_ref[...]
    s16 = state.astype(BF16)
    ws = bmm(jnp.concatenate([w, q * eg], axis=1).astype(BF16), s16)
    v_new = u - ws[:, :c]
    qk = bmm(q.astype(BF16), k16, _dot_nt) * decay
    v16 = v_new.astype(BF16)
    out = ws[:, c:] + bmm(qk.astype(BF16), v16)
    ke = (k * jnp.exp(gl - gc)).astype(BF16)
    s_ref[...] = state * jnp.exp(gl) + bmm(ke, v16, _dot_tn)
    o_refs = (of_ref, ob_ref)
    for m, (d, b, h) in enumerate(chains):
        o_refs[d][b, :, h * DN_HEAD_DIM:(h + 1) * DN_HEAD_DIM] = out[m].astype(BF16)


def deltanet_scan(qn, kn, vn, gb, n_lat):
    batch, t_all, _ = qn.shape
    c = DN_CHUNK
    n_chunks = t_all // c
    n_lat_chunks = n_lat // c
    n_ctx_chunks = n_chunks - n_lat_chunks

    def fwd(s):
        return jnp.where(s < n_ctx_chunks, n_lat_chunks + s, s - n_ctx_chunks)

    def bwd(s):
        return n_chunks - 1 - s

    tok_f = pl.BlockSpec((batch, c, DN_WIDTH), lambda s: (0, fwd(s), 0))
    tok_b = pl.BlockSpec((batch, c, DN_WIDTH), lambda s: (0, bwd(s), 0))
    out = jax.ShapeDtypeStruct((batch, t_all, DN_WIDTH), BF16)
    return pl.pallas_call(
        functools.partial(_dnscan_kernel, batch),
        grid=(n_chunks,),
        in_specs=[tok_f, tok_f, tok_f, pl.BlockSpec((batch, c, 128), lambda s: (0, fwd(s), 0)),
                  tok_b, tok_b, tok_b, pl.BlockSpec((batch, c, 128), lambda s: (0, bwd(s), 1))],
        out_specs=[tok_f, tok_b],
        out_shape=[out, out],
        scratch_shapes=[pltpu.VMEM((2 * batch * DN_HEADS, DN_HEAD_DIM, DN_HEAD_DIM), F32)],
        compiler_params=_params("arbitrary"),
        name="deltanet_scan",
    )(qn, kn, vn, gb, qn, kn, vn, gb)


def _attn_kernel(tq, tk, nk, lam_init, q_ref, k_ref, v_ref, lv_ref, gain_ref, o_ref,
                 qs_ref, vp_ref, s0_ref, s1_ref, m_ref, acc_ref):
    dv = DA_V_DIM

    @pl.when(pl.program_id(2) == 0)
    def _():
        vp_ref[:, 0:dv] = v_ref[0]
        vp_ref[:, dv:2 * dv] = jnp.ones((vp_ref.shape[0], dv), BF16)

    q = q_ref[0]
    lane = lax.broadcasted_iota(jnp.int32, q.shape, 1)
    zero = jnp.zeros_like(q)
    qs_ref[0:tq, :] = jnp.where(lane < DA_QK_DIM, q, zero)
    qs_ref[tq:2 * tq, :] = jnp.where(lane >= DA_QK_DIM, q, zero)
    m_ref[...] = jnp.full_like(m_ref, -jnp.inf)
    acc_ref[...] = jnp.zeros_like(acc_ref)
    s_refs = (s0_ref, s1_ref)
    halves = (slice(0, tq), slice(tq, 2 * tq))
    for half in halves:
        s0_ref[half, :] = _dot_nt(qs_ref[half, :], k_ref[0, 0:tk, :])

    def step(j, cur, prefetch):
        if prefetch:
            k_next = k_ref[0, pl.ds(pl.multiple_of((j + 1) * tk, tk), tk), :]
            s_refs[1 - cur][...] = _dot_nt(qs_ref[...], k_next)
        s = s_refs[cur][...]
        m_old = m_ref[...]
        m_new = jnp.maximum(m_old, jnp.max(s, axis=-1, keepdims=True))
        alpha = jnp.exp2(m_old - m_new)
        p = jnp.concatenate([jnp.exp2((s[:, t:t + dv] - m_new).astype(BF16)) for t in range(0, tk, dv)], axis=1)
        alpha2 = jnp.concatenate([alpha, alpha], axis=1)
        vp = vp_ref[pl.ds(pl.multiple_of(j * tk, tk), tk), :]
        if prefetch:
            acc_ref[...] = alpha2 * acc_ref[...] + _dot(p, vp)
        else:
            for half in halves:
                acc_ref[half, :] = alpha2[half] * acc_ref[half, :] + _dot(p[half], vp)
        m_ref[...] = m_new

    n_pairs = (nk - 1) // 2
    if n_pairs > 0:
        def body(i, carry):
            step(2 * i, 0, True)
            step(2 * i + 1, 1, True)
            return carry
        lax.fori_loop(0, n_pairs, body, 0)
    for j in range(2 * n_pairs, nk):
        step(j, j % 2, j + 1 < nk)

    lv = lv_ref[...]
    lam = (jnp.exp(jnp.sum(lv[0:1] * lv[1:2], axis=-1, keepdims=True))
           - jnp.exp(jnp.sum(lv[2:3] * lv[3:4], axis=-1, keepdims=True)) + lam_init)
    acc = acc_ref[...]
    o = acc[:, :dv] / acc[:, dv:]
    o = o[:tq] - lam * o[tq:]
    o_ref[0] = (_rms(o) * gain_ref[...] * (1.0 - lam_init)).astype(BF16)


def diff_attention(q, k, v, lam_vecs, gain, lam_init, q_rows, q_off, k_rows, k_off, tq, tk):
    batch = q.shape[0]
    nq, nk = q_rows // tq, k_rows // tk
    qo, ko = q_off // tq, k_off // k_rows
    kv = pl.BlockSpec((1, k_rows, DA_V_DIM), lambda b, h, i: (b, ko, h))
    return pl.pallas_call(
        functools.partial(_attn_kernel, tq, tk, nk, lam_init),
        grid=(batch, DA_HEADS, nq),
        in_specs=[pl.BlockSpec((1, tq, DA_V_DIM), lambda b, h, i: (b, qo + i, h)), kv, kv,
                  pl.BlockSpec(lam_vecs.shape, lambda b, h, i: (0, 0)),
                  pl.BlockSpec(gain.shape, lambda b, h, i: (0, 0))],
        out_specs=pl.BlockSpec((1, tq, DA_V_DIM), lambda b, h, i: (b, i, h)),
        out_shape=jax.ShapeDtypeStruct((batch, q_rows, DA_WIDTH), BF16),
        scratch_shapes=[pltpu.VMEM((2 * tq, DA_V_DIM), BF16), pltpu.VMEM((k_rows, 2 * DA_V_DIM), BF16),
                        pltpu.VMEM((2 * tq, tk), F32), pltpu.VMEM((2 * tq, tk), F32), pltpu.VMEM((2 * tq, DA_V_DIM), F32),
                        pltpu.VMEM((2 * tq, 2 * DA_V_DIM), F32)],
        compiler_params=_params("parallel", "parallel", "arbitrary"),
        name="diff_attention",
    )(q, k, v, lam_vecs, gain)


def _merge_mlp_kernel(n_lat_tiles, batch, n_x, has_ctx, final, *refs):
    x_refs = refs[:n_x]
    refs = refs[n_x:]
    if has_ctx:
        (mod_ref, yf_ref, yfc_ref, odnf_ref, odnb_ref, z_ref, oda_ref, odac_ref, gate_ref, dng_ref,
         wf_ref, wdn_ref, wda_ref, wo_ref, gain2_ref, w1_ref, w2_ref, fg_ref, o_ref) = refs
    else:
        (mod_ref, yf_ref, odnf_ref, odnb_ref, z_ref, oda_ref, gate_ref, dng_ref,
         wf_ref, wdn_ref, wda_ref, wo_ref, gain2_ref, w1_ref, w2_ref, fg_ref, o_ref) = refs
    mod = _mod_row(mod_ref, n_lat_tiles, batch)
    g1 = mod[:, 2 * D_MODEL:3 * D_MODEL]
    yf = yf_ref[0]
    oda = oda_ref[0]
    if has_ctx:
        is_ctx = pl.program_id(1) >= n_lat_tiles
        yf = jnp.where(is_ctx, yfc_ref[0], yf)
        oda = jnp.where(is_ctx, odac_ref[0], oda)
    o = odnf_ref[0].astype(F32) + odnb_ref[0].astype(F32)
    z = z_ref[0].astype(F32)
    dn_parts = []
    for hd in range(DN_HEADS):
        sl = slice(hd * DN_HEAD_DIM, (hd + 1) * DN_HEAD_DIM)
        dn_parts.append(_rms(o[:, sl]) * dng_ref[...] * _silu(z[:, sl]))
    odn = jnp.concatenate(dn_parts, axis=1).astype(BF16)
    gate = gate_ref[0].astype(F32)
    merged = (gate[:, 0:D_MODEL] * _dot(yf, wf_ref[...])
              + gate[:, D_MODEL:2 * D_MODEL] * _dot(odn, wdn_ref[...])
              + gate[:, 2 * D_MODEL:] * _dot(oda, wda_ref[...]))
    y = _dot(merged.astype(BF16), wo_ref[...])
    x = _token_tile(x_refs, n_lat_tiles) + g1 * y
    sh = mod[:, 3 * D_MODEL:4 * D_MODEL]
    sc = mod[:, 4 * D_MODEL:5 * D_MODEL]
    g2 = mod[:, 5 * D_MODEL:]
    h = ((_rms(x) * gain2_ref[...]) * (1.0 + sc) + sh).astype(BF16)
    a = jnp.maximum(_dot(h, w1_ref[...]), 0.0)
    out = x + g2 * _dot((a * a).astype(BF16), w2_ref[...])
    if final:
        out = _rms(out) * fg_ref[...]
    o_ref[0] = out


def merge_and_mlp(xs, mod, yf, yf_ctx, odn, z, oda, oda_ctx, gates, dn_gain, w_f, w_dn, w_da, w_o,
                  gain2, w1, w2, final_gain, n_lat_tiles, n_tiles, final):
    batch, _, d = xs[0].shape
    tm = ROW_TILE
    has_ctx = yf_ctx is not None
    lat_last = n_lat_tiles - 1

    def rows(width):
        return pl.BlockSpec((1, tm, width), lambda b, i: (b, i, 0))

    def lat_rows(width):
        return pl.BlockSpec((1, tm, width), lambda b, i: (b, jnp.minimum(i, lat_last), 0))

    def ctx_rows(width):
        return pl.BlockSpec((1, tm, width), lambda b, i: (b, jnp.maximum(i - n_lat_tiles, 0), 0))

    args = list(xs) + [mod, yf]
    specs = _token_specs(xs, n_lat_tiles, d) + [_resident(mod.shape), lat_rows(F_WIDTH)]
    if has_ctx:
        args.append(yf_ctx)
        specs.append(ctx_rows(F_WIDTH))
    args += [odn[0], odn[1], z, oda]
    specs += [rows(DN_WIDTH), rows(DN_WIDTH), rows(DN_WIDTH), lat_rows(DA_WIDTH)]
    if has_ctx:
        args.append(oda_ctx)
        specs.append(ctx_rows(DA_WIDTH))
    weights = [dn_gain, w_f, w_dn, w_da, w_o, gain2, w1, w2, final_gain]
    args += [gates] + weights
    specs += [rows(3 * D_MODEL)] + [_resident(w.shape) for w in weights]
    return pl.pallas_call(
        functools.partial(_merge_mlp_kernel, n_lat_tiles, batch, len(xs), has_ctx, final),
        grid=(batch, n_tiles),
        in_specs=specs,
        out_specs=rows(d),
        out_shape=jax.ShapeDtypeStruct((batch, n_tiles * tm, d), F32),
        compiler_params=_params("parallel", "parallel"),
        name="merge_and_mlp",
    )(*args)


def _arrange_w_in(w_in):
    splits = np.cumsum([F_WIDTH, 3 * DN_WIDTH, DN_WIDTH, 4 * DN_HEADS, DA_WIDTH, DA_WIDTH, DA_WIDTH])
    w_uf, w_qkv, w_z, w_ab, w_q, w_k, w_v, w_g = jnp.split(w_in, splits, axis=1)
    h = DN_HEADS
    pad = jnp.zeros((w_in.shape[0], 128 - 2 * h), w_in.dtype)
    ab = jnp.concatenate([w_ab[:, 0:h], w_ab[:, 2 * h:3 * h], pad,
                          w_ab[:, h:2 * h], w_ab[:, 3 * h:4 * h], pad], axis=1)
    cols = [w_uf, w_qkv, w_z, w_q, w_k, w_v, w_g, ab]
    return jnp.concatenate(cols, axis=1).astype(BF16)


def _rope_tables(n_lat, n_ctx):
    t = jnp.arange(n_lat)
    row = (t // GRID_W).astype(F32)
    col = (t % GRID_W).astype(F32)
    n_freq = DA_QK_DIM // 4
    inv_freq = ROPE_THETA ** (-jnp.arange(n_freq, dtype=F32) / n_freq)
    ang_r = row[:, None] * inv_freq[None, :]
    ang_c = col[:, None] * inv_freq[None, :]
    ang = jnp.concatenate([ang_r, ang_r, ang_c, ang_c], axis=1)
    sign = jnp.asarray(np.where((np.arange(DA_QK_DIM) & 16) == 0, -1.0, 1.0), F32)
    cos = jnp.concatenate([jnp.tile(jnp.cos(ang), (1, 2)), jnp.ones((n_ctx, 128), F32)], axis=0)
    sin = jnp.concatenate([jnp.tile(jnp.sin(ang) * sign, (1, 2)), jnp.zeros((n_ctx, 128), F32)], axis=0)
    return cos, sin


def _decay_params(a_log, dt_bias):
    h = DN_HEADS
    out = jnp.zeros((8, 256), F32)
    for d in range(2):
        out = out.at[0, d * 128 + h:d * 128 + 2 * h].set(a_log[d])
        out = out.at[1, d * 128 + h:d * 128 + 2 * h].set(dt_bias[d])
    return out


def kernel(x, c, ctx, c_ctx, norm1, norm2, w_ada, b_ada, w_in, conv_w, a_log, dt_bias, dn_gain, lam_vecs,
           da_gain, w_f, w_dn, w_da, w_o, w_mlp1, w_mlp2, final_norm):
    batch, n_lat, d = x.shape
    n_ctx = ctx.shape[1]
    depth = w_in.shape[0]
    t_all = n_lat + n_ctx
    n_lat_tiles = n_lat // ROW_TILE
    n_tiles = t_all // ROW_TILE

    cvec = jnp.zeros((8, d), F32).at[:batch].set(c).at[batch].set(c_ctx)
    mods = ada_modulation(cvec, w_ada, b_ada)
    cos_t, sin_t = _rope_tables(n_lat, n_ctx)
    xs = (x, ctx)
    final_gain = final_norm.reshape(1, d)

    for l in range(depth):
        last = l == depth - 1
        lam_init = 0.8 - 0.6 * math.exp(-0.3 * l)
        mod = mods[l]
        w_big = _arrange_w_in(w_in[l])
        abp = _decay_params(a_log[l], dt_bias[l])
        uf, qkv, z, q, k, v, gates, gb = input_projection(
            xs, mod, norm1[l].reshape(1, d), w_big, cos_t, sin_t, abp, n_lat_tiles)

        yf = fourier_latent(uf, n_lat)
        conv_w8 = jnp.zeros((8, 3 * DN_WIDTH), F32).at[:DN_CONV].set(conv_w[l])
        qn, kn, vn = deltanet_prep(qkv, conv_w8, n_lat_tiles)
        odn = deltanet_scan(qn, kn, vn, gb, n_lat)
        gain_da = da_gain[l].reshape(1, DA_V_DIM)
        tk = next(t for t in (768, 512, 256) if t_all % t == 0)
        tq = min(ATTN_Q_TILE, n_lat)
        oda = diff_attention(q, k, v, lam_vecs[l], gain_da, lam_init, n_lat, 0, t_all, 0, tq, tk)
        if last:
            yf_ctx = oda_ctx = None
            tiles = n_lat_tiles
        else:
            yf_ctx = fourier_context(uf, n_lat, n_ctx)
            oda_ctx = diff_attention(q, k, v, lam_vecs[l], gain_da, lam_init, n_ctx, n_lat, n_ctx, n_lat,
                                     n_ctx, n_ctx)
            tiles = n_tiles
        out = merge_and_mlp(xs, mod, yf, yf_ctx, odn, z, oda, oda_ctx, gates, dn_gain[l].reshape(1, DN_HEAD_DIM),
                            w_f[l].astype(BF16), w_dn[l].astype(BF16), w_da[l].astype(BF16), w_o[l].astype(BF16),
                            norm2[l].reshape(1, d), w_mlp1[l].astype(BF16), w_mlp2[l].astype(BF16), final_gain,
                            n_lat_tiles, tiles, last)
        xs = (out,)
    return xs[0]
```

```python
import functools
import math

import numpy as np
import jax
import jax.numpy as jnp
from jax import lax
from jax.experimental import pallas as pl
from jax.experimental.pallas import tpu as pltpu

F32 = jnp.float32
BF16 = jnp.bfloat16

D_MODEL = 1024
GRID_W = 64
F_GROUPS = 4
F_GROUP_DIM = 128
F_WIDTH = F_GROUPS * F_GROUP_DIM
DN_HEADS = 4
DN_HEAD_DIM = 128
DN_WIDTH = DN_HEADS * DN_HEAD_DIM
DN_CONV = 5
DN_CHUNK = 64
DA_HEADS = 4
DA_QK_DIM = 64
DA_V_DIM = 2 * DA_QK_DIM
DA_WIDTH = DA_HEADS * DA_V_DIM
ROPE_THETA = 10000.0
D_FF = 4 * D_MODEL
NORM_EPS = 1e-6
LOG2_E = math.log2(math.e)

ROW_TILE = 256
ATTN_Q_TILE = 1024
CONV_HALO = 16
VMEM_LIMIT = 56 * 1024 * 1024

C_UF = 0
C_QKV = C_UF + F_WIDTH
C_Z = C_QKV + 3 * DN_WIDTH
C_Q = C_Z + DN_WIDTH
C_K = C_Q + DA_WIDTH
C_V = C_K + DA_WIDTH
C_GATE = C_V + DA_WIDTH
C_AB = C_GATE + 3 * D_MODEL
W_IN_COLS = C_AB + 256


def _dot(a, b):
    return jnp.dot(a, b, preferred_element_type=F32)


def _dot_f32(a, b):
    return jnp.dot(a, b, preferred_element_type=F32, precision=lax.Precision.HIGHEST)


def _dot_nt(a, b):
    return lax.dot_general(a, b, (((1,), (1,)), ((), ())), preferred_element_type=F32)


def _dot_tn(a, b):
    return lax.dot_general(a, b, (((0,), (0,)), ((), ())), preferred_element_type=F32)


def _silu(x):
    return x * jax.nn.sigmoid(x)


def _softplus(x):
    return jnp.maximum(x, 0.0) + jnp.log1p(jnp.exp(-jnp.abs(x)))


def _params(*sem):
    return pltpu.CompilerParams(dimension_semantics=sem, vmem_limit_bytes=VMEM_LIMIT)


def _resident(shape):
    nd = len(shape)
    return pl.BlockSpec(shape, lambda *_: (0,) * nd, pipeline_mode=pl.Buffered(1))


def _ada_kernel(c_ref, w_ref, b_ref, o_ref):
    a = _silu(c_ref[...])
    o_ref[0] = _dot_f32(a, w_ref[0]) + b_ref[0]


def ada_modulation(cvec, w_ada, b_ada):
    depth, d, n = w_ada.shape
    tn = 1536
    return pl.pallas_call(
        _ada_kernel,
        grid=(depth, n // tn),
        in_specs=[pl.BlockSpec((8, d), lambda l, j: (0, 0)),
                  pl.BlockSpec((1, d, tn), lambda l, j: (l, 0, j)),
                  pl.BlockSpec((1, 1, tn), lambda l, j: (l, 0, j))],
        out_specs=pl.BlockSpec((1, 8, tn), lambda l, j: (l, 0, j)),
        out_shape=jax.ShapeDtypeStruct((depth, 8, n), F32),
        compiler_params=_params("arbitrary", "arbitrary"),
        name="ada_modulation",
    )(cvec, w_ada, b_ada.reshape(depth, 1, n))


def _mod_row(mod_ref, n_lat_tiles, batch):
    i = pl.program_id(1)
    b = pl.program_id(0)
    r = jnp.where(i >= n_lat_tiles, batch, b)
    return mod_ref[pl.ds(r, 1), :]


def _rms(x):
    return x * lax.rsqrt(jnp.mean(x * x, axis=-1, keepdims=True) + NORM_EPS)


def _token_specs(arrays, n_lat_tiles, width):
    tm = ROW_TILE
    if len(arrays) == 1:
        return [pl.BlockSpec((1, tm, width), lambda b, i: (b, i, 0))]
    return [pl.BlockSpec((1, tm, width), lambda b, i: (b, jnp.minimum(i, n_lat_tiles - 1), 0)),
            pl.BlockSpec((1, tm, width), lambda b, i: (b, jnp.maximum(i - n_lat_tiles, 0), 0))]


def _token_tile(refs, n_lat_tiles):
    if len(refs) == 1:
        return refs[0][0]
    return jnp.where(pl.program_id(1) >= n_lat_tiles, refs[1][0], refs[0][0])


def _rotary(x, cos, sin_signed):
    width = x.shape[-1]
    lane = lax.broadcasted_iota(jnp.int32, x.shape, 1)
    partner = jnp.where((lane & 16) == 0, pltpu.roll(x, width - 16, 1), pltpu.roll(x, 16, 1))
    return x * cos + partner * sin_signed


def _inproj_kernel(n_lat_tiles, batch, n_x, *refs):
    x_refs = refs[:n_x]
    (mod_ref, gain_ref, w_ref, cos_ref, sin_ref, abp_ref,
     uf_ref, qkv_ref, z_ref, q_ref, k_ref, v_ref, gate_ref, gb_ref) = refs[n_x:]
    mod = _mod_row(mod_ref, n_lat_tiles, batch)
    sh = mod[:, 0:D_MODEL]
    sc = mod[:, D_MODEL:2 * D_MODEL]
    h = (_rms(_token_tile(x_refs, n_lat_tiles)) * gain_ref[...]) * (1.0 + sc) + sh
    hb = h.astype(BF16)

    def proj(c0, width):
        return _dot(hb, w_ref[:, c0:c0 + width])

    uf_ref[0] = proj(C_UF, F_WIDTH).astype(BF16)
    for j in range(3):
        qkv_ref[0, :, j * DN_WIDTH:(j + 1) * DN_WIDTH] = proj(C_QKV + j * DN_WIDTH, DN_WIDTH).astype(BF16)
    z_ref[0] = proj(C_Z, DN_WIDTH).astype(BF16)
    cos = jnp.tile(cos_ref[...], (1, DA_HEADS))
    sin = jnp.tile(sin_ref[...], (1, DA_HEADS))
    q = _rotary(proj(C_Q, DA_WIDTH), cos, sin)
    q_ref[0] = (q * (DA_QK_DIM ** -0.5 * LOG2_E)).astype(BF16)
    k_ref[0] = _rotary(proj(C_K, DA_WIDTH), cos, sin).astype(BF16)
    v_ref[0] = proj(C_V, DA_WIDTH).astype(BF16)
    for j in range(3 * D_MODEL // 512):
        g = proj(C_GATE + j * 512, 512)
        gate_ref[0, :, j * 512:(j + 1) * 512] = jax.nn.sigmoid(g).astype(BF16)
    ab = proj(C_AB, 256)
    abp = abp_ref[...]
    lane = lax.broadcasted_iota(jnp.int32, ab.shape, 1) % 128
    beta = jax.nn.sigmoid(ab)
    g = -jnp.exp(abp[0:1, :]) * _softplus(ab + abp[1:2, :])
    gb_ref[0] = jnp.where(lane < DN_HEADS, beta, g)


def input_projection(xs, mod, gain, w_big, cos_t, sin_t, abp, n_lat_tiles):
    batch, _, d = xs[0].shape
    t_all = sum(a.shape[1] for a in xs)
    n_tiles = t_all // ROW_TILE
    tm = ROW_TILE

    def rows(width, dtype):
        return (pl.BlockSpec((1, tm, width), lambda b, i: (b, i, 0)),
                jax.ShapeDtypeStruct((batch, t_all, width), dtype))

    outs = [rows(F_WIDTH, BF16), rows(3 * DN_WIDTH, BF16), rows(DN_WIDTH, BF16), rows(DA_WIDTH, BF16),
            rows(DA_WIDTH, BF16), rows(DA_WIDTH, BF16), rows(3 * D_MODEL, BF16), rows(256, F32)]
    return pl.pallas_call(
        functools.partial(_inproj_kernel, n_lat_tiles, batch, len(xs)),
        grid=(batch, n_tiles),
        in_specs=_token_specs(xs, n_lat_tiles, d) + [
            _resident(mod.shape), _resident(gain.shape), _resident(w_big.shape),
            pl.BlockSpec((tm, 128), lambda b, i: (i, 0)),
            pl.BlockSpec((tm, 128), lambda b, i: (i, 0)),
            _resident(abp.shape)],
        out_specs=[o[0] for o in outs],
        out_shape=[o[1] for o in outs],
        compiler_params=_params("parallel", "parallel"),
        name="input_projection",
    )(*xs, mod, gain, w_big, cos_t, sin_t, abp)


def _dft_tables(n_pos):
    n1 = n_pos // GRID_W
    n2 = GRID_W
    a = np.arange(n1)
    ang1 = 2.0 * np.pi * ((a[:, None] * a[None, :]) % n1) / n1
    stage1 = np.concatenate([np.cos(ang1), -np.sin(ang1)], axis=0)
    k1 = np.arange(n1)[:, None, None]
    k2 = np.arange(n2)[None, :, None]
    m = np.arange(n2)[None, None, :]
    ang2 = 2.0 * np.pi * ((m * (k1 + n1 * k2)) % n_pos) / n_pos
    c2, s2 = np.cos(ang2), np.sin(ang2)
    stage2 = np.concatenate([np.concatenate([c2, s2], axis=2),
                             np.concatenate([-s2, c2], axis=2)], axis=1)
    return stage1, stage2


def _channel_table(n_pos):
    c = np.arange(F_GROUP_DIM)
    ang = 2.0 * np.pi * ((c[:, None] * c[None, :]) % F_GROUP_DIM) / F_GROUP_DIM
    scale = 1.0 / math.sqrt(n_pos * F_GROUP_DIM)
    return np.concatenate([np.cos(ang), np.sin(ang)], axis=0) * scale


def _channel_mix(pr, pi, chan):
    outs = []
    for g in range(F_GROUPS):
        sl = slice(g * F_GROUP_DIM, (g + 1) * F_GROUP_DIM)
        pg = jnp.concatenate([pr[:, sl], pi[:, sl]], axis=1).astype(BF16)
        outs.append(_dot(pg, chan))
    return jnp.concatenate(outs, axis=1)


def _fft1_kernel(n1, f_ref, x_ref, ar_ref, ai_ref):
    a = _dot(f_ref[...].astype(BF16), x_ref[0])
    ar_ref[0] = a[:n1].astype(BF16)
    ai_ref[0] = a[n1:].astype(BF16)


def _fft2_kernel(kb, m_ref, ar_ref, ai_ref, chan_ref, o_ref):
    chan = chan_ref[...].astype(BF16)
    a = [jnp.concatenate([ar_ref[0, j], ai_ref[0, j]], axis=0) for j in range(kb)]
    p = [_dot(m_ref[j].astype(BF16), a[j]) for j in range(kb)]
    pg = [[jnp.concatenate([p[j][:GRID_W, g * F_GROUP_DIM:(g + 1) * F_GROUP_DIM],
                            p[j][GRID_W:, g * F_GROUP_DIM:(g + 1) * F_GROUP_DIM]], axis=1).astype(BF16)
           for g in range(F_GROUPS)] for j in range(kb)]
    y = [[_dot(pg[j][g], chan) for g in range(F_GROUPS)] for j in range(kb)]
    for j in range(kb):
        o_ref[0, :, j * F_WIDTH:(j + 1) * F_WIDTH] = jnp.concatenate(y[j], axis=1).astype(BF16)


def fourier_latent(uf_all, n_pos):
    batch, t_all, _ = uf_all.shape
    n1 = n_pos // GRID_W
    stage1, stage2 = _dft_tables(n_pos)
    f1 = jnp.asarray(stage1, F32)
    m2 = jnp.asarray(stage2, F32)
    chan = jnp.asarray(_channel_table(n_pos), F32)
    wide = GRID_W * F_WIDTH
    x2 = uf_all.reshape(batch, t_all // GRID_W, wide)
    tn = 4096
    ar, ai = pl.pallas_call(
        functools.partial(_fft1_kernel, n1),
        grid=(batch, wide // tn),
        in_specs=[_resident(f1.shape),
                  pl.BlockSpec((1, n1, tn), lambda b, j: (b, 0, j))],
        out_specs=[pl.BlockSpec((1, n1, tn), lambda b, j: (b, 0, j))] * 2,
        out_shape=[jax.ShapeDtypeStruct((batch, n1, wide), BF16)] * 2,
        compiler_params=_params("parallel", "parallel"),
        name="fourier_stage1",
    )(f1, x2)
    ar = ar.reshape(batch, n1, GRID_W, F_WIDTH)
    ai = ai.reshape(batch, n1, GRID_W, F_WIDTH)
    kb = 8
    y = pl.pallas_call(
        functools.partial(_fft2_kernel, kb),
        grid=(batch, n1 // kb),
        in_specs=[pl.BlockSpec((kb, 2 * GRID_W, 2 * GRID_W), lambda b, i: (i, 0, 0)),
                  pl.BlockSpec((1, kb, GRID_W, F_WIDTH), lambda b, i: (b, i, 0, 0)),
                  pl.BlockSpec((1, kb, GRID_W, F_WIDTH), lambda b, i: (b, i, 0, 0)),
                  _resident(chan.shape)],
        out_specs=pl.BlockSpec((1, GRID_W, kb * F_WIDTH), lambda b, i: (b, 0, i)),
        out_shape=jax.ShapeDtypeStruct((batch, GRID_W, n1 * F_WIDTH), BF16),
        compiler_params=_params("parallel", "parallel"),
        name="fourier_stage2",
    )(m2, ar, ai, chan)
    return y.reshape(batch, n_pos, F_WIDTH)


def _fft_ctx_kernel(tc, f_ref, x_ref, chan_ref, o_ref):
    p = _dot(f_ref[...].astype(BF16), x_ref[0])
    o_ref[0] = _channel_mix(p[:tc], p[tc:], chan_ref[...].astype(BF16)).astype(BF16)


def fourier_context(uf_all, n_pos, tc):
    batch = uf_all.shape[0]
    a = np.arange(tc)
    ang = 2.0 * np.pi * ((a[:, None] * a[None, :]) % tc) / tc
    f = jnp.asarray(np.concatenate([np.cos(ang), -np.sin(ang)], axis=0), F32)
    chan = jnp.asarray(_channel_table(tc), F32)
    return pl.pallas_call(
        functools.partial(_fft_ctx_kernel, tc),
        grid=(batch,),
        in_specs=[_resident(f.shape),
                  pl.BlockSpec((1, tc, F_WIDTH), lambda b: (b, n_pos // tc, 0)),
                  _resident(chan.shape)],
        out_specs=pl.BlockSpec((1, tc, F_WIDTH), lambda b: (b, 0, 0)),
        out_shape=jax.ShapeDtypeStruct((batch, tc, F_WIDTH), BF16),
        compiler_params=_params("parallel"),
        name="fourier_context",
    )(f, uf_all, chan)


def _dnprep_kernel(n_lat_tiles, n_tiles, cur_ref, prev_ref, next_ref, w_ref, q_ref, k_ref, v_ref, ext_ref):
    i = pl.program_id(1)
    tm = ROW_TILE
    h = CONV_HALO
    prev_ok = jnp.logical_and(i != 0, i != n_lat_tiles)
    next_ok = jnp.logical_and(i != n_lat_tiles - 1, i != n_tiles - 1)
    cur = cur_ref[0]
    prev = jnp.where(prev_ok, prev_ref[0], jnp.zeros_like(prev_ref[0]))
    nxt = jnp.where(next_ok, next_ref[0], jnp.zeros_like(next_ref[0]))
    pad = DN_CONV // 2
    t_out = lax.broadcasted_iota(jnp.int32, (tm, tm), 0)
    t_in = lax.broadcasted_iota(jnp.int32, (tm, tm), 1)
    acc = cur.astype(F32) * w_ref[pad:pad + 1, :]
    for j in range(DN_CONV):
        if j != pad:
            acc = acc + _dot((t_in == t_out + (j - pad)).astype(BF16), cur) * w_ref[j:j + 1, :]
    ext_ref[...] = acc
    r_out = lax.broadcasted_iota(jnp.int32, (8, h), 0)
    r_in = lax.broadcasted_iota(jnp.int32, (8, h), 1)
    top = bot = None
    for j in range(pad):
        term = _dot((r_in == r_out + (h + j - pad)).astype(BF16), prev) * w_ref[j:j + 1, :]
        top = term if top is None else top + term
    for j in range(pad + 1, DN_CONV):
        term = _dot((r_in == r_out + (j - pad - 8)).astype(BF16), nxt) * w_ref[j:j + 1, :]
        bot = term if bot is None else bot + term
    ext_ref[0:8, :] += top
    ext_ref[tm - 8:tm, :] += bot
    u = _silu(ext_ref[...])

    def l2n(block):
        return block * lax.rsqrt(jnp.sum(block * block, axis=-1, keepdims=True) + NORM_EPS)

    for hd in range(DN_HEADS):
        sl = slice(hd * DN_HEAD_DIM, (hd + 1) * DN_HEAD_DIM)
        q = u[:, hd * DN_HEAD_DIM:(hd + 1) * DN_HEAD_DIM]
        k = u[:, DN_WIDTH + hd * DN_HEAD_DIM:DN_WIDTH + (hd + 1) * DN_HEAD_DIM]
        q_ref[0, :, sl] = (l2n(q) * (DN_HEAD_DIM ** -0.5)).astype(BF16)
        k_ref[0, :, sl] = l2n(k).astype(BF16)
    v_ref[0] = u[:, 2 * DN_WIDTH:].astype(BF16)


def deltanet_prep(qkv, conv_w8, n_lat_tiles):
    batch, t_all, width = qkv.shape
    tm = ROW_TILE
    n_tiles = t_all // tm
    per = tm // CONV_HALO
    last = t_all // CONV_HALO - 1
    out = (pl.BlockSpec((1, tm, DN_WIDTH), lambda b, i: (b, i, 0)),
           jax.ShapeDtypeStruct((batch, t_all, DN_WIDTH), BF16))
    return pl.pallas_call(
        functools.partial(_dnprep_kernel, n_lat_tiles, n_tiles),
        grid=(batch, n_tiles),
        in_specs=[pl.BlockSpec((1, tm, width), lambda b, i: (b, i, 0)),
                  pl.BlockSpec((1, CONV_HALO, width), lambda b, i: (b, jnp.maximum(i * per - 1, 0), 0)),
                  pl.BlockSpec((1, CONV_HALO, width), lambda b, i: (b, jnp.minimum((i + 1) * per, last), 0)),
                  _resident(conv_w8.shape)],
        out_specs=[out[0]] * 3,
        out_shape=[out[1]] * 3,
        scratch_shapes=[pltpu.VMEM((tm, width), F32)],
        compiler_params=_params("parallel", "parallel"),
        name="deltanet_prep",
    )(qkv, qkv, qkv, conv_w8)


def _dnscan_kernel(batch, qf_ref, kf_ref, vf_ref, gf_ref, qb_ref, kb_ref, vb_ref, gb_ref, of_ref, ob_ref, s_ref):
    c = DN_CHUNK
    hh = DN_HEADS
    chains = [(d, b, h) for d in range(2) for b in range(batch) for h in range(hh)]
    n = len(chains)
    half = n // 2

    @pl.when(pl.program_id(0) == 0)
    def _():
        s_ref[...] = jnp.zeros_like(s_ref)

    row = lax.broadcasted_iota(jnp.int32, (c, 128), 0)

    def prefix_sum(g, d):
        x = g
        shift = 1
        while shift < c:
            if d == 0:
                x = x + jnp.where(row >= shift, pltpu.roll(x, shift, 0), 0.0)
            else:
                x = x + jnp.where(row < c - shift, pltpu.roll(x, c - shift, 0), 0.0)
            shift *= 2
        return x

    g_refs = (gf_ref, gb_ref)
    gcum, grow, gtot, graw = {}, {}, {}, {}
    for d in range(2):
        for b in range(batch):
            g = g_refs[d][b]
            graw[d, b] = g
            gcum[d, b] = prefix_sum(g, d)
            grow[d, b] = gcum[d, b].T
            gtot[d, b] = jnp.sum(g, axis=0, keepdims=True)

    def per_chain(fn):
        return jnp.stack([fn(d, b, h) for d, b, h in chains])

    def tile(refs):
        return per_chain(lambda d, b, h: refs[d][b, :, h * DN_HEAD_DIM:(h + 1) * DN_HEAD_DIM]).astype(F32)

    q = tile((qf_ref, qb_ref))
    k = tile((kf_ref, kb_ref))
    v = tile((vf_ref, vb_ref))
    beta = per_chain(lambda d, b, h: graw[d, b][:, h:h + 1])
    gc = per_chain(lambda d, b, h: gcum[d, b][:, hh + h:hh + h + 1])
    gr = per_chain(lambda d, b, h: grow[d, b][hh + h:hh + h + 1, :])
    gl = per_chain(lambda d, b, h: gtot[d, b][:, hh + h:hh + h + 1])

    def bmm(a, bm, fn=_dot):
        return jnp.stack([fn(a[m], bm[m]) for m in range(n)])

    ch = lax.broadcasted_iota(jnp.int32, (n, c, c), 0)
    ii = lax.broadcasted_iota(jnp.int32, (n, c, c), 1)
    jj = lax.broadcasted_iota(jnp.int32, (n, c, c), 2)
    diff = jnp.where(ch < half, ii - jj, jj - ii)
    incl = diff >= 0
    strict = diff > 0
    eye = (ii == jj).astype(F32)

    decay = jnp.exp(jnp.where(incl, gc - gr, -jnp.inf))
    kbeta = k * beta
    k16 = k.astype(BF16)
    lmat = jnp.where(strict, bmm(kbeta.astype(BF16), k16, _dot_nt) * decay, 0.0)
    blk = lambda s: (ii // s) == (jj // s)
    t = eye - jnp.where(blk(2), lmat, 0.0)
    s = 2
    while s < c:
        ls = jnp.where(jnp.logical_and(blk(2 * s), jnp.logical_not(blk(s))), lmat, 0.0)
        t16 = t.astype(BF16)
        t = t - bmm(bmm(t16, ls.astype(BF16)).astype(BF16), t16)
        s *= 2
    eg = jnp.exp(gc)
    rhs = jnp.concatenate([v * beta, kbeta * eg], axis=2).astype(BF16)
    uw = bmm(t.astype(BF16), rhs)
    u = uw[:, :, :DN_HEAD_DIM]
    w = uw[:, :, DN_HEAD_DIM:]
    state = s_ref[...]
    s16 = state.astype(BF16)
    ws = bmm(jnp.concatenate([w, q * eg], axis=1).astype(BF16), s16)
    v_new = u - ws[:, :c]
    qk = bmm(q.astype(BF16), k16, _dot_nt) * decay
    v16 = v_new.astype(BF16)
    out = ws[:, c:] + bmm(qk.astype(BF16), v16)
    ke = (k * jnp.exp(gl - gc)).astype(BF16)
    s_ref[...] = state * jnp.exp(gl) + bmm(ke, v16, _dot_tn)
    o_refs = (of_ref, ob_ref)
    for m, (d, b, h) in enumerate(chains):
        o_refs[d][b, :, h * DN_HEAD_DIM:(h + 1) * DN_HEAD_DIM] = out[m].astype(BF16)


def deltanet_scan(qn, kn, vn, gb, n_lat):
    batch, t_all, _ = qn.shape
    c = DN_CHUNK
    n_chunks = t_all // c
    n_lat_chunks = n_lat // c
    n_ctx_chunks = n_chunks - n_lat_chunks

    def fwd(s):
        return jnp.where(s < n_ctx_chunks, n_lat_chunks + s, s - n_ctx_chunks)

    def bwd(s):
        return n_chunks - 1 - s

    tok_f = pl.BlockSpec((batch, c, DN_WIDTH), lambda s: (0, fwd(s), 0))
    tok_b = pl.BlockSpec((batch, c, DN_WIDTH), lambda s: (0, bwd(s), 0))
    out = jax.ShapeDtypeStruct((batch, t_all, DN_WIDTH), BF16)
    return pl.pallas_call(
        functools.partial(_dnscan_kernel, batch),
        grid=(n_chunks,),
        in_specs=[tok_f, tok_f, tok_f, pl.BlockSpec((batch, c, 128), lambda s: (0, fwd(s), 0)),
                  tok_b, tok_b, tok_b, pl.BlockSpec((batch, c, 128), lambda s: (0, bwd(s), 1))],
        out_specs=[tok_f, tok_b],
        out_shape=[out, out],
        scratch_shapes=[pltpu.VMEM((2 * batch * DN_HEADS, DN_HEAD_DIM, DN_HEAD_DIM), F32)],
        compiler_params=_params("arbitrary"),
        name="deltanet_scan",
    )(qn, kn, vn, gb, qn, kn, vn, gb)


def _attn_kernel(tq, tk, nk, lam_init, q_ref, qn_ref, k_ref, v_ref, lv_ref, gain_ref, o_ref,
                 qs_ref, qsn_ref, vp_ref, sp_ref, s0_ref, s1_ref, m_ref, acc_ref):
    dv = DA_V_DIM
    halves = (slice(0, tq), slice(tq, 2 * tq))
    first_tile = pl.program_id(2) == 0

    def stack_queries(src_ref):
        q = src_ref[0]
        lane = lax.broadcasted_iota(jnp.int32, q.shape, 1)
        zero = jnp.zeros_like(q)
        qsn_ref[0:tq, :] = jnp.where(lane < DA_QK_DIM, q, zero)
        qsn_ref[tq:2 * tq, :] = jnp.where(lane >= DA_QK_DIM, q, zero)

    def softmax_pv(j, src_ref, split):
        s = src_ref[...]
        m_old = m_ref[...]
        m_new = jnp.maximum(m_old, jnp.max(s, axis=-1, keepdims=True))
        alpha = jnp.exp2(m_old - m_new)
        p = jnp.concatenate([jnp.exp2((s[:, t:t + dv] - m_new).astype(BF16)) for t in range(0, tk, dv)], axis=1)
        alpha2 = jnp.concatenate([alpha, alpha], axis=1)
        vp = vp_ref[pl.ds(pl.multiple_of(j * tk, tk), tk), :]
        if split:
            for half in halves:
                acc_ref[half, :] = alpha2[half] * acc_ref[half, :] + _dot(p[half], vp)
        else:
            acc_ref[...] = alpha2 * acc_ref[...] + _dot(p, vp)
        m_ref[...] = m_new

    def step(j, src_ref, dst_ref):
        k_next = k_ref[0, pl.ds(pl.multiple_of((j + 1) * tk, tk), tk), :]
        dst_ref[...] = _dot_nt(qs_ref[...], k_next)
        softmax_pv(j, src_ref, False)

    @pl.when(first_tile)
    def _():
        vp_ref[:, 0:dv] = v_ref[0]
        vp_ref[:, dv:2 * dv] = jnp.ones((vp_ref.shape[0], dv), BF16)
        stack_queries(q_ref)
        for half in halves:
            sp_ref[half, :] = _dot_nt(qsn_ref[half, :], k_ref[0, 0:tk, :])

    qs_ref[...] = qsn_ref[...]
    m_ref[...] = jnp.full_like(m_ref, -jnp.inf)
    acc_ref[...] = jnp.zeros_like(acc_ref)

    if nk == 1:
        softmax_pv(0, sp_ref, True)
    else:
        ping = (s1_ref, s0_ref)
        step(0, sp_ref, s0_ref)
        n_pairs = (nk - 2) // 2
        if n_pairs > 0:
            def body(i, carry):
                step(2 * i + 1, s0_ref, s1_ref)
                step(2 * i + 2, s1_ref, s0_ref)
                return carry
            lax.fori_loop(0, n_pairs, body, 0)
        for j in range(2 * n_pairs + 1, nk - 1):
            step(j, ping[j % 2], ping[(j + 1) % 2])
        stack_queries(qn_ref)
        sp_ref[...] = _dot_nt(qsn_ref[...], k_ref[0, 0:tk, :])
        softmax_pv(nk - 1, ping[(nk - 1) % 2], False)

    lv = lv_ref[...]
    lam = (jnp.exp(jnp.sum(lv[0:1] * lv[1:2], axis=-1, keepdims=True))
           - jnp.exp(jnp.sum(lv[2:3] * lv[3:4], axis=-1, keepdims=True)) + lam_init)
    acc = acc_ref[...]
    o = acc[:, :dv] / acc[:, dv:]
    o = o[:tq] - lam * o[tq:]
    o_ref[0] = (_rms(o) * gain_ref[...] * (1.0 - lam_init)).astype(BF16)


def diff_attention(q, k, v, lam_vecs, gain, lam_init, q_rows, q_off, k_rows, k_off, tq, tk):
    batch = q.shape[0]
    nq, nk = q_rows // tq, k_rows // tk
    qo, ko = q_off // tq, k_off // k_rows
    assert nk > 1 or nq == 1
    kv = pl.BlockSpec((1, k_rows, DA_V_DIM), lambda b, h, i: (b, ko, h))
    scores = pltpu.VMEM((2 * tq, tk), F32)
    stacked_q = pltpu.VMEM((2 * tq, DA_V_DIM), BF16)
    return pl.pallas_call(
        functools.partial(_attn_kernel, tq, tk, nk, lam_init),
        grid=(batch, DA_HEADS, nq),
        in_specs=[pl.BlockSpec((1, tq, DA_V_DIM), lambda b, h, i: (b, qo + i, h)),
                  pl.BlockSpec((1, tq, DA_V_DIM), lambda b, h, i: (b, qo + jnp.minimum(i + 1, nq - 1), h)),
                  kv, kv,
                  pl.BlockSpec(lam_vecs.shape, lambda b, h, i: (0, 0)),
                  pl.BlockSpec(gain.shape, lambda b, h, i: (0, 0))],
        out_specs=pl.BlockSpec((1, tq, DA_V_DIM), lambda b, h, i: (b, i, h)),
        out_shape=jax.ShapeDtypeStruct((batch, q_rows, DA_WIDTH), BF16),
        scratch_shapes=[stacked_q, stacked_q, pltpu.VMEM((k_rows, 2 * DA_V_DIM), BF16),
                        scores, scores, scores, pltpu.VMEM((2 * tq, DA_V_DIM), F32),
                        pltpu.VMEM((2 * tq, 2 * DA_V_DIM), F32)],
        compiler_params=_params("parallel", "parallel", "arbitrary"),
        name="diff_attention",
    )(q, q, k, v, lam_vecs, gain)


def _merge_mlp_kernel(n_lat_tiles, batch, n_x, has_ctx, final, *refs):
    x_refs = refs[:n_x]
    refs = refs[n_x:]
    if has_ctx:
        (mod_ref, yf_ref, yfc_ref, odnf_ref, odnb_ref, z_ref, oda_ref, odac_ref, gate_ref, dng_ref,
         wf_ref, wdn_ref, wda_ref, wo_ref, gain2_ref, w1_ref, w2_ref, fg_ref, o_ref) = refs
    else:
        (mod_ref, yf_ref, odnf_ref, odnb_ref, z_ref, oda_ref, gate_ref, dng_ref,
         wf_ref, wdn_ref, wda_ref, wo_ref, gain2_ref, w1_ref, w2_ref, fg_ref, o_ref) = refs
    mod = _mod_row(mod_ref, n_lat_tiles, batch)
    g1 = mod[:, 2 * D_MODEL:3 * D_MODEL]
    yf = yf_ref[0]
    oda = oda_ref[0]
    if has_ctx:
        is_ctx = pl.program_id(1) >= n_lat_tiles
        yf = jnp.where(is_ctx, yfc_ref[0], yf)
        oda = jnp.where(is_ctx, odac_ref[0], oda)
    o = odnf_ref[0].astype(F32) + odnb_ref[0].astype(F32)
    z = z_ref[0].astype(F32)
    dn_parts = []
    for hd in range(DN_HEADS):
        sl = slice(hd * DN_HEAD_DIM, (hd + 1) * DN_HEAD_DIM)
        dn_parts.append(_rms(o[:, sl]) * dng_ref[...] * _silu(z[:, sl]))
    odn = jnp.concatenate(dn_parts, axis=1).astype(BF16)
    gate = gate_ref[0].astype(F32)
    merged = (gate[:, 0:D_MODEL] * _dot(yf, wf_ref[...])
              + gate[:, D_MODEL:2 * D_MODEL] * _dot(odn, wdn_ref[...])
              + gate[:, 2 * D_MODEL:] * _dot(oda, wda_ref[...]))
    y = _dot(merged.astype(BF16), wo_ref[...])
    x = _token_tile(x_refs, n_lat_tiles) + g1 * y
    sh = mod[:, 3 * D_MODEL:4 * D_MODEL]
    sc = mod[:, 4 * D_MODEL:5 * D_MODEL]
    g2 = mod[:, 5 * D_MODEL:]
    h = ((_rms(x) * gain2_ref[...]) * (1.0 + sc) + sh).astype(BF16)
    a = jnp.maximum(_dot(h, w1_ref[...]), 0.0)
    out = x + g2 * _dot((a * a).astype(BF16), w2_ref[...])
    if final:
        out = _rms(out) * fg_ref[...]
    o_ref[0] = out


def merge_and_mlp(xs, mod, yf, yf_ctx, odn, z, oda, oda_ctx, gates, dn_gain, w_f, w_dn, w_da, w_o,
                  gain2, w1, w2, final_gain, n_lat_tiles, n_tiles, final):
    batch, _, d = xs[0].shape
    tm = ROW_TILE
    has_ctx = yf_ctx is not None
    lat_last = n_lat_tiles - 1

    def rows(width):
        return pl.BlockSpec((1, tm, width), lambda b, i: (b, i, 0))

    def lat_rows(width):
        return pl.BlockSpec((1, tm, width), lambda b, i: (b, jnp.minimum(i, lat_last), 0))

    def ctx_rows(width):
        return pl.BlockSpec((1, tm, width), lambda b, i: (b, jnp.maximum(i - n_lat_tiles, 0), 0))

    args = list(xs) + [mod, yf]
    specs = _token_specs(xs, n_lat_tiles, d) + [_resident(mod.shape), lat_rows(F_WIDTH)]
    if has_ctx:
        args.append(yf_ctx)
        specs.append(ctx_rows(F_WIDTH))
    args += [odn[0], odn[1], z, oda]
    specs += [rows(DN_WIDTH), rows(DN_WIDTH), rows(DN_WIDTH), lat_rows(DA_WIDTH)]
    if has_ctx:
        args.append(oda_ctx)
        specs.append(ctx_rows(DA_WIDTH))
    weights = [dn_gain, w_f, w_dn, w_da, w_o, gain2, w1, w2, final_gain]
    args += [gates] + weights
    specs += [rows(3 * D_MODEL)] + [_resident(w.shape) for w in weights]
    return pl.pallas_call(
        functools.partial(_merge_mlp_kernel, n_lat_tiles, batch, len(xs), has_ctx, final),
        grid=(batch, n_tiles),
        in_specs=specs,
        out_specs=rows(d),
        out_shape=jax.ShapeDtypeStruct((batch, n_tiles * tm, d), F32),
        compiler_params=_params("parallel", "parallel"),
        name="merge_and_mlp",
    )(*args)


def _arrange_w_in(w_in):
    splits = np.cumsum([F_WIDTH, 3 * DN_WIDTH, DN_WIDTH, 4 * DN_HEADS, DA_WIDTH, DA_WIDTH, DA_WIDTH])
    w_uf, w_qkv, w_z, w_ab, w_q, w_k, w_v, w_g = jnp.split(w_in, splits, axis=1)
    h = DN_HEADS
    pad = jnp.zeros((w_in.shape[0], 128 - 2 * h), w_in.dtype)
    ab = jnp.concatenate([w_ab[:, 0:h], w_ab[:, 2 * h:3 * h], pad,
                          w_ab[:, h:2 * h], w_ab[:, 3 * h:4 * h], pad], axis=1)
    cols = [w_uf, w_qkv, w_z, w_q, w_k, w_v, w_g, ab]
    return jnp.concatenate(cols, axis=1).astype(BF16)


def _rope_tables(n_lat, n_ctx):
    t = jnp.arange(n_lat)
    row = (t // GRID_W).astype(F32)
    col = (t % GRID_W).astype(F32)
    n_freq = DA_QK_DIM // 4
    inv_freq = ROPE_THETA ** (-jnp.arange(n_freq, dtype=F32) / n_freq)
    ang_r = row[:, None] * inv_freq[None, :]
    ang_c = col[:, None] * inv_freq[None, :]
    ang = jnp.concatenate([ang_r, ang_r, ang_c, ang_c], axis=1)
    sign = jnp.asarray(np.where((np.arange(DA_QK_DIM) & 16) == 0, -1.0, 1.0), F32)
    cos = jnp.concatenate([jnp.tile(jnp.cos(ang), (1, 2)), jnp.ones((n_ctx, 128), F32)], axis=0)
    sin = jnp.concatenate([jnp.tile(jnp.sin(ang) * sign, (1, 2)), jnp.zeros((n_ctx, 128), F32)], axis=0)
    return cos, sin


def _decay_params(a_log, dt_bias):
    h = DN_HEADS
    out = jnp.zeros((8, 256), F32)
    for d in range(2):
        out = out.at[0, d * 128 + h:d * 128 + 2 * h].set(a_log[d])
        out = out.at[1, d * 128 + h:d * 128 + 2 * h].set(dt_bias[d])
    return out


def kernel(x, c, ctx, c_ctx, norm1, norm2, w_ada, b_ada, w_in, conv_w, a_log, dt_bias, dn_gain, lam_vecs,
           da_gain, w_f, w_dn, w_da, w_o, w_mlp1, w_mlp2, final_norm):
    batch, n_lat, d = x.shape
    n_ctx = ctx.shape[1]
    depth = w_in.shape[0]
    t_all = n_lat + n_ctx
    n_lat_tiles = n_lat // ROW_TILE
    n_tiles = t_all // ROW_TILE

    cvec = jnp.zeros((8, d), F32).at[:batch].set(c).at[batch].set(c_ctx)
    mods = ada_modulation(cvec, w_ada, b_ada)
    cos_t, sin_t = _rope_tables(n_lat, n_ctx)
    xs = (x, ctx)
    final_gain = final_norm.reshape(1, d)

    for l in range(depth):
        last = l == depth - 1
        lam_init = 0.8 - 0.6 * math.exp(-0.3 * l)
        mod = mods[l]
        w_big = _arrange_w_in(w_in[l])
        abp = _decay_params(a_log[l], dt_bias[l])
        uf, qkv, z, q, k, v, gates, gb = input_projection(
            xs, mod, norm1[l].reshape(1, d), w_big, cos_t, sin_t, abp, n_lat_tiles)

        yf = fourier_latent(uf, n_lat)
        conv_w8 = jnp.zeros((8, 3 * DN_WIDTH), F32).at[:DN_CONV].set(conv_w[l])
        qn, kn, vn = deltanet_prep(qkv, conv_w8, n_lat_tiles)
        odn = deltanet_scan(qn, kn, vn, gb, n_lat)
        gain_da = da_gain[l].reshape(1, DA_V_DIM)
        tk = next(t for t in (768, 512, 256) if t_all % t == 0)
        tq = min(ATTN_Q_TILE, n_lat)
        oda = diff_attention(q, k, v, lam_vecs[l], gain_da, lam_init, n_lat, 0, t_all, 0, tq, tk)
        if last:
            yf_ctx = oda_ctx = None
            tiles = n_lat_tiles
        else:
            yf_ctx = fourier_context(uf, n_lat, n_ctx)
            oda_ctx = diff_attention(q, k, v, lam_vecs[l], gain_da, lam_init, n_ctx, n_lat, n_ctx, n_lat,
                                     n_ctx, n_ctx)
            tiles = n_tiles
        out = merge_and_mlp(xs, mod, yf, yf_ctx, odn, z, oda, oda_ctx, gates, dn_gain[l].reshape(1, DN_HEAD_DIM),
                            w_f[l].astype(BF16), w_dn[l].astype(BF16), w_da[l].astype(BF16), w_o[l].astype(BF16),
                            norm2[l].reshape(1, d), w_mlp1[l].astype(BF16), w_mlp2[l].astype(BF16), final_gain,
                            n_lat_tiles, tiles, last)
        xs = (out,)
    return xs[0]
```

```python
import functools
import math

import numpy as np
import jax
import jax.numpy as jnp
from jax import lax
from jax.experimental import pallas as pl
from jax.experimental.pallas import tpu as pltpu

F32 = jnp.float32
BF16 = jnp.bfloat16

D_MODEL = 1024
GRID_W = 64
F_GROUPS = 4
F_GROUP_DIM = 128
F_WIDTH = F_GROUPS * F_GROUP_DIM
DN_HEADS = 4
DN_HEAD_DIM = 128
DN_WIDTH = DN_HEADS * DN_HEAD_DIM
DN_CONV = 5
DN_CHUNK = 64
DA_HEADS = 4
DA_QK_DIM = 64
DA_V_DIM = 2 * DA_QK_DIM
DA_WIDTH = DA_HEADS * DA_V_DIM
ROPE_THETA = 10000.0
D_FF = 4 * D_MODEL
NORM_EPS = 1e-6
LOG2_E = math.log2(math.e)

ROW_TILE = 256
ATTN_Q_TILE = 1024
CONV_HALO = 16
VMEM_LIMIT = 56 * 1024 * 1024

C_UF = 0
C_QKV = C_UF + F_WIDTH
C_Z = C_QKV + 3 * DN_WIDTH
C_Q = C_Z + DN_WIDTH
C_K = C_Q + DA_WIDTH
C_V = C_K + DA_WIDTH
C_GATE = C_V + DA_WIDTH
C_AB = C_GATE + 3 * D_MODEL
W_IN_COLS = C_AB + 256


def _dot(a, b):
    return jnp.dot(a, b, preferred_element_type=F32)


def _dot_f32(a, b):
    return jnp.dot(a, b, preferred_element_type=F32, precision=lax.Precision.HIGHEST)


def _dot_nt(a, b):
    return lax.dot_general(a, b, (((1,), (1,)), ((), ())), preferred_element_type=F32)


def _dot_tn(a, b):
    return lax.dot_general(a, b, (((0,), (0,)), ((), ())), preferred_element_type=F32)


def _silu(x):
    return x * jax.nn.sigmoid(x)


def _softplus(x):
    return jnp.maximum(x, 0.0) + jnp.log1p(jnp.exp(-jnp.abs(x)))


def _params(*sem):
    return pltpu.CompilerParams(dimension_semantics=sem, vmem_limit_bytes=VMEM_LIMIT)


def _resident(shape):
    nd = len(shape)
    return pl.BlockSpec(shape, lambda *_: (0,) * nd, pipeline_mode=pl.Buffered(1))


def _param_spec(p):
    if isinstance(p, tuple):
        arr, layer = p
        return pl.BlockSpec((None,) + arr.shape[1:], lambda *_: (layer,) + (0,) * (arr.ndim - 1),
                            pipeline_mode=pl.Buffered(1))
    return _resident(p.shape)


def _param_arg(p):
    return p[0] if isinstance(p, tuple) else p


def _ada_kernel(c_ref, w_ref, b_ref, o_ref):
    a = _silu(c_ref[...])
    o_ref[0] = _dot_f32(a, w_ref[0]) + b_ref[0]


def ada_modulation(cvec, w_ada, b_ada):
    depth, d, n = w_ada.shape
    tn = 1536
    return pl.pallas_call(
        _ada_kernel,
        grid=(depth, n // tn),
        in_specs=[pl.BlockSpec((8, d), lambda l, j: (0, 0)),
                  pl.BlockSpec((1, d, tn), lambda l, j: (l, 0, j)),
                  pl.BlockSpec((1, 1, tn), lambda l, j: (l, 0, j))],
        out_specs=pl.BlockSpec((1, 8, tn), lambda l, j: (l, 0, j)),
        out_shape=jax.ShapeDtypeStruct((depth, 8, n), F32),
        compiler_params=_params("arbitrary", "arbitrary"),
        name="ada_modulation",
    )(cvec, w_ada, b_ada.reshape(depth, 1, n))


def _mod_row(mod_ref, n_lat_tiles, batch):
    i = pl.program_id(1)
    b = pl.program_id(0)
    r = jnp.where(i >= n_lat_tiles, batch, b)
    return mod_ref[pl.ds(r, 1), :]


def _rms(x):
    return x * lax.rsqrt(jnp.mean(x * x, axis=-1, keepdims=True) + NORM_EPS)


def _token_specs(arrays, n_lat_tiles, width):
    tm = ROW_TILE
    if len(arrays) == 1:
        return [pl.BlockSpec((1, tm, width), lambda b, i: (b, i, 0))]
    return [pl.BlockSpec((1, tm, width), lambda b, i: (b, jnp.minimum(i, n_lat_tiles - 1), 0)),
            pl.BlockSpec((1, tm, width), lambda b, i: (b, jnp.maximum(i - n_lat_tiles, 0), 0))]


def _token_tile(refs, n_lat_tiles):
    if len(refs) == 1:
        return refs[0][0]
    return jnp.where(pl.program_id(1) >= n_lat_tiles, refs[1][0], refs[0][0])


def _rotary(x, cos, sin_signed):
    width = x.shape[-1]
    lane = lax.broadcasted_iota(jnp.int32, x.shape, 1)
    partner = jnp.where((lane & 16) == 0, pltpu.roll(x, width - 16, 1), pltpu.roll(x, 16, 1))
    return x * cos + partner * sin_signed


def _inproj_kernel(n_lat_tiles, batch, n_x, *refs):
    x_refs = refs[:n_x]
    (mod_ref, gain_ref, w_ref, cos_ref, sin_ref, abp_ref,
     uf_ref, qkv_ref, z_ref, q_ref, k_ref, v_ref, gate_ref, gb_ref) = refs[n_x:]
    mod = _mod_row(mod_ref, n_lat_tiles, batch)
    sh = mod[:, 0:D_MODEL]
    sc = mod[:, D_MODEL:2 * D_MODEL]
    h = (_rms(_token_tile(x_refs, n_lat_tiles)) * gain_ref[...]) * (1.0 + sc) + sh
    hb = h.astype(BF16)

    def proj(c0, width):
        return _dot(hb, w_ref[:, c0:c0 + width])

    uf_ref[0] = proj(C_UF, F_WIDTH).astype(BF16)
    for j in range(3):
        qkv_ref[0, :, j * DN_WIDTH:(j + 1) * DN_WIDTH] = proj(C_QKV + j * DN_WIDTH, DN_WIDTH).astype(BF16)
    z_ref[0] = proj(C_Z, DN_WIDTH).astype(BF16)
    cos = jnp.tile(cos_ref[...], (1, DA_HEADS))
    sin = jnp.tile(sin_ref[...], (1, DA_HEADS))
    q = _rotary(proj(C_Q, DA_WIDTH), cos, sin)
    q_ref[0] = (q * (DA_QK_DIM ** -0.5 * LOG2_E)).astype(BF16)
    k_ref[0] = _rotary(proj(C_K, DA_WIDTH), cos, sin).astype(BF16)
    v_ref[0] = proj(C_V, DA_WIDTH).astype(BF16)
    for j in range(3 * D_MODEL // 512):
        g = proj(C_GATE + j * 512, 512)
        gate_ref[0, :, j * 512:(j + 1) * 512] = jax.nn.sigmoid(g).astype(BF16)
    ab = proj(C_AB, 256)
    abp = abp_ref[...]
    lane = lax.broadcasted_iota(jnp.int32, ab.shape, 1) % 128
    beta = jax.nn.sigmoid(ab)
    g = -jnp.exp(abp[0:1, :]) * _softplus(ab + abp[1:2, :])
    gb_ref[0] = jnp.where(lane < DN_HEADS, beta, g)


def input_projection(xs, mod, gain, w_big, cos_t, sin_t, abp, n_lat_tiles):
    batch, _, d = xs[0].shape
    t_all = sum(a.shape[1] for a in xs)
    n_tiles = t_all // ROW_TILE
    tm = ROW_TILE

    def rows(width, dtype):
        return (pl.BlockSpec((1, tm, width), lambda b, i: (b, i, 0)),
                jax.ShapeDtypeStruct((batch, t_all, width), dtype))

    outs = [rows(F_WIDTH, BF16), rows(3 * DN_WIDTH, BF16), rows(DN_WIDTH, BF16), rows(DA_WIDTH, BF16),
            rows(DA_WIDTH, BF16), rows(DA_WIDTH, BF16), rows(3 * D_MODEL, BF16), rows(256, F32)]
    return pl.pallas_call(
        functools.partial(_inproj_kernel, n_lat_tiles, batch, len(xs)),
        grid=(batch, n_tiles),
        in_specs=_token_specs(xs, n_lat_tiles, d) + [
            _resident(mod.shape), _resident(gain.shape), _param_spec(w_big),
            pl.BlockSpec((tm, 128), lambda b, i: (i, 0)),
            pl.BlockSpec((tm, 128), lambda b, i: (i, 0)),
            _resident(abp.shape)],
        out_specs=[o[0] for o in outs],
        out_shape=[o[1] for o in outs],
        compiler_params=_params("parallel", "parallel"),
        name="input_projection",
    )(*xs, mod, gain, _param_arg(w_big), cos_t, sin_t, abp)


def _dft_tables(n_pos):
    n1 = n_pos // GRID_W
    n2 = GRID_W
    a = np.arange(n1)
    ang1 = 2.0 * np.pi * ((a[:, None] * a[None, :]) % n1) / n1
    stage1 = np.concatenate([np.cos(ang1), -np.sin(ang1)], axis=0)
    k1 = np.arange(n1)[:, None, None]
    k2 = np.arange(n2)[None, :, None]
    m = np.arange(n2)[None, None, :]
    ang2 = 2.0 * np.pi * ((m * (k1 + n1 * k2)) % n_pos) / n_pos
    c2, s2 = np.cos(ang2), np.sin(ang2)
    stage2 = np.concatenate([np.concatenate([c2, s2], axis=2),
                             np.concatenate([-s2, c2], axis=2)], axis=1)
    return stage1, stage2


def _channel_table(n_pos):
    c = np.arange(F_GROUP_DIM)
    ang = 2.0 * np.pi * ((c[:, None] * c[None, :]) % F_GROUP_DIM) / F_GROUP_DIM
    scale = 1.0 / math.sqrt(n_pos * F_GROUP_DIM)
    return np.concatenate([np.cos(ang), np.sin(ang)], axis=0) * scale


def _channel_mix(pr, pi, chan):
    outs = []
    for g in range(F_GROUPS):
        sl = slice(g * F_GROUP_DIM, (g + 1) * F_GROUP_DIM)
        pg = jnp.concatenate([pr[:, sl], pi[:, sl]], axis=1).astype(BF16)
        outs.append(_dot(pg, chan))
    return jnp.concatenate(outs, axis=1)


def _fft1_kernel(n1, f_ref, x_ref, ar_ref, ai_ref):
    a = _dot(f_ref[...].astype(BF16), x_ref[0])
    ar_ref[0] = a[:n1].astype(BF16)
    ai_ref[0] = a[n1:].astype(BF16)


def _fft2_kernel(kb, m_ref, ar_ref, ai_ref, chan_ref, o_ref):
    chan = chan_ref[...].astype(BF16)
    a = [jnp.concatenate([ar_ref[0, j], ai_ref[0, j]], axis=0) for j in range(kb)]
    p = [_dot(m_ref[j].astype(BF16), a[j]) for j in range(kb)]
    pg = [[jnp.concatenate([p[j][:GRID_W, g * F_GROUP_DIM:(g + 1) * F_GROUP_DIM],
                            p[j][GRID_W:, g * F_GROUP_DIM:(g + 1) * F_GROUP_DIM]], axis=1).astype(BF16)
           for g in range(F_GROUPS)] for j in range(kb)]
    y = [[_dot(pg[j][g], chan) for g in range(F_GROUPS)] for j in range(kb)]
    for j in range(kb):
        o_ref[0, :, j * F_WIDTH:(j + 1) * F_WIDTH] = jnp.concatenate(y[j], axis=1).astype(BF16)


def fourier_latent(uf_all, n_pos):
    batch, t_all, _ = uf_all.shape
    n1 = n_pos // GRID_W
    stage1, stage2 = _dft_tables(n_pos)
    f1 = jnp.asarray(stage1, F32)
    m2 = jnp.asarray(stage2, F32)
    chan = jnp.asarray(_channel_table(n_pos), F32)
    wide = GRID_W * F_WIDTH
    x2 = uf_all.reshape(batch, t_all // GRID_W, wide)
    tn = 4096
    ar, ai = pl.pallas_call(
        functools.partial(_fft1_kernel, n1),
        grid=(batch, wide // tn),
        in_specs=[_resident(f1.shape),
                  pl.BlockSpec((1, n1, tn), lambda b, j: (b, 0, j))],
        out_specs=[pl.BlockSpec((1, n1, tn), lambda b, j: (b, 0, j))] * 2,
        out_shape=[jax.ShapeDtypeStruct((batch, n1, wide), BF16)] * 2,
        compiler_params=_params("parallel", "parallel"),
        name="fourier_stage1",
    )(f1, x2)
    ar = ar.reshape(batch, n1, GRID_W, F_WIDTH)
    ai = ai.reshape(batch, n1, GRID_W, F_WIDTH)
    kb = 8
    y = pl.pallas_call(
        functools.partial(_fft2_kernel, kb),
        grid=(batch, n1 // kb),
        in_specs=[pl.BlockSpec((kb, 2 * GRID_W, 2 * GRID_W), lambda b, i: (i, 0, 0)),
                  pl.BlockSpec((1, kb, GRID_W, F_WIDTH), lambda b, i: (b, i, 0, 0)),
                  pl.BlockSpec((1, kb, GRID_W, F_WIDTH), lambda b, i: (b, i, 0, 0)),
                  _resident(chan.shape)],
        out_specs=pl.BlockSpec((1, GRID_W, kb * F_WIDTH), lambda b, i: (b, 0, i)),
        out_shape=jax.ShapeDtypeStruct((batch, GRID_W, n1 * F_WIDTH), BF16),
        compiler_params=_params("parallel", "parallel"),
        name="fourier_stage2",
    )(m2, ar, ai, chan)
    return y.reshape(batch, n_pos, F_WIDTH)


def _fft_ctx_kernel(tc, f_ref, x_ref, chan_ref, o_ref):
    p = _dot(f_ref[...].astype(BF16), x_ref[0])
    o_ref[0] = _channel_mix(p[:tc], p[tc:], chan_ref[...].astype(BF16)).astype(BF16)


def fourier_context(uf_all, n_pos, tc):
    batch = uf_all.shape[0]
    a = np.arange(tc)
    ang = 2.0 * np.pi * ((a[:, None] * a[None, :]) % tc) / tc
    f = jnp.asarray(np.concatenate([np.cos(ang), -np.sin(ang)], axis=0), F32)
    chan = jnp.asarray(_channel_table(tc), F32)
    return pl.pallas_call(
        functools.partial(_fft_ctx_kernel, tc),
        grid=(batch,),
        in_specs=[_resident(f.shape),
                  pl.BlockSpec((1, tc, F_WIDTH), lambda b: (b, n_pos // tc, 0)),
                  _resident(chan.shape)],
        out_specs=pl.BlockSpec((1, tc, F_WIDTH), lambda b: (b, 0, 0)),
        out_shape=jax.ShapeDtypeStruct((batch, tc, F_WIDTH), BF16),
        compiler_params=_params("parallel"),
        name="fourier_context",
    )(f, uf_all, chan)


def _dnprep_kernel(n_lat_tiles, n_tiles, cur_ref, prev_ref, next_ref, w_ref, q_ref, k_ref, v_ref):
    i = pl.program_id(1)
    tm = ROW_TILE
    h = CONV_HALO
    prev_ok = jnp.logical_and(i != 0, i != n_lat_tiles)
    next_ok = jnp.logical_and(i != n_lat_tiles - 1, i != n_tiles - 1)
    pad = DN_CONV // 2
    t_out = lax.broadcasted_iota(jnp.int32, (tm, tm), 0)
    t_in = lax.broadcasted_iota(jnp.int32, (tm, tm), 1)
    r_out = lax.broadcasted_iota(jnp.int32, (8, h), 0)
    r_in = lax.broadcasted_iota(jnp.int32, (8, h), 1)
    shift = {j: (t_in == t_out + (j - pad)).astype(BF16) for j in range(DN_CONV) if j != pad}
    from_prev = {j: (r_in == r_out + (h + j - pad)).astype(BF16) for j in range(pad)}
    from_next = {j: (r_in == r_out + (j - pad - 8)).astype(BF16) for j in range(pad + 1, DN_CONV)}

    def l2n(block):
        return block * lax.rsqrt(jnp.sum(block * block, axis=-1, keepdims=True) + NORM_EPS)

    block_w = 2 * DN_HEAD_DIM
    for c0 in range(0, 3 * DN_WIDTH, block_w):
        cols = slice(c0, c0 + block_w)
        cur = cur_ref[0, :, cols]
        prev = jnp.where(prev_ok, prev_ref[0, :, cols], jnp.zeros((h, block_w), BF16))
        nxt = jnp.where(next_ok, next_ref[0, :, cols], jnp.zeros((h, block_w), BF16))
        acc = cur.astype(F32) * w_ref[pad:pad + 1, cols]
        for j, s_mat in shift.items():
            acc = acc + _dot(s_mat, cur) * w_ref[j:j + 1, cols]
        top = sum(_dot(m, prev) * w_ref[j:j + 1, cols] for j, m in from_prev.items())
        bot = sum(_dot(m, nxt) * w_ref[j:j + 1, cols] for j, m in from_next.items())
        acc = jnp.concatenate([acc[:8] + top, acc[8:tm - 8], acc[tm - 8:] + bot], axis=0)
        u = _silu(acc)
        for half in range(2):
            blk = u[:, half * DN_HEAD_DIM:(half + 1) * DN_HEAD_DIM]
            c = c0 + half * DN_HEAD_DIM
            if c < DN_WIDTH:
                q_ref[0, :, c:c + DN_HEAD_DIM] = (l2n(blk) * (DN_HEAD_DIM ** -0.5)).astype(BF16)
            elif c < 2 * DN_WIDTH:
                k_ref[0, :, c - DN_WIDTH:c - DN_WIDTH + DN_HEAD_DIM] = l2n(blk).astype(BF16)
            else:
                v_ref[0, :, c - 2 * DN_WIDTH:c - 2 * DN_WIDTH + DN_HEAD_DIM] = blk.astype(BF16)


def deltanet_prep(qkv, conv_w8, n_lat_tiles):
    batch, t_all, width = qkv.shape
    tm = ROW_TILE
    n_tiles = t_all // tm
    per = tm // CONV_HALO
    last = t_all // CONV_HALO - 1
    out = (pl.BlockSpec((1, tm, DN_WIDTH), lambda b, i: (b, i, 0)),
           jax.ShapeDtypeStruct((batch, t_all, DN_WIDTH), BF16))
    return pl.pallas_call(
        functools.partial(_dnprep_kernel, n_lat_tiles, n_tiles),
        grid=(batch, n_tiles),
        in_specs=[pl.BlockSpec((1, tm, width), lambda b, i: (b, i, 0)),
                  pl.BlockSpec((1, CONV_HALO, width), lambda b, i: (b, jnp.maximum(i * per - 1, 0), 0)),
                  pl.BlockSpec((1, CONV_HALO, width), lambda b, i: (b, jnp.minimum((i + 1) * per, last), 0)),
                  _resident(conv_w8.shape)],
        out_specs=[out[0]] * 3,
        out_shape=[out[1]] * 3,
        compiler_params=_params("parallel", "parallel"),
        name="deltanet_prep",
    )(qkv, qkv, qkv, conv_w8)


def _dnscan_kernel(batch, qf_ref, kf_ref, vf_ref, gf_ref, qb_ref, kb_ref, vb_ref, gb_ref, of_ref, ob_ref, s_ref):
    c = DN_CHUNK
    hh = DN_HEADS
    chains = [(d, b, h) for d in range(2) for b in range(batch) for h in range(hh)]
    n = len(chains)
    half = n // 2

    @pl.when(pl.program_id(0) == 0)
    def _():
        s_ref[...] = jnp.zeros_like(s_ref)

    row = lax.broadcasted_iota(jnp.int32, (c, 128), 0)

    def prefix_sum(g, d):
        x = g
        shift = 1
        while shift < c:
            if d == 0:
                x = x + jnp.where(row >= shift, pltpu.roll(x, shift, 0), 0.0)
            else:
                x = x + jnp.where(row < c - shift, pltpu.roll(x, c - shift, 0), 0.0)
            shift *= 2
        return x

    g_refs = (gf_ref, gb_ref)
    gcum, grow, gtot, graw = {}, {}, {}, {}
    for d in range(2):
        for b in range(batch):
            g = g_refs[d][b]
            graw[d, b] = g
            gcum[d, b] = prefix_sum(g, d)
            grow[d, b] = gcum[d, b].T
            gtot[d, b] = jnp.sum(g, axis=0, keepdims=True)

    def per_chain(fn):
        return jnp.stack([fn(d, b, h) for d, b, h in chains])

    def tile(refs):
        return per_chain(lambda d, b, h: refs[d][b, :, h * DN_HEAD_DIM:(h + 1) * DN_HEAD_DIM]).astype(F32)

    q = tile((qf_ref, qb_ref))
    k = tile((kf_ref, kb_ref))
    v = tile((vf_ref, vb_ref))
    beta = per_chain(lambda d, b, h: graw[d, b][:, h:h + 1])
    gc = per_chain(lambda d, b, h: gcum[d, b][:, hh + h:hh + h + 1])
    gr = per_chain(lambda d, b, h: grow[d, b][hh + h:hh + h + 1, :])
    gl = per_chain(lambda d, b, h: gtot[d, b][:, hh + h:hh + h + 1])

    def bmm(a, bm, fn=_dot):
        return jnp.stack([fn(a[m], bm[m]) for m in range(n)])

    ch = lax.broadcasted_iota(jnp.int32, (n, c, c), 0)
    ii = lax.broadcasted_iota(jnp.int32, (n, c, c), 1)
    jj = lax.broadcasted_iota(jnp.int32, (n, c, c), 2)
    diff = jnp.where(ch < half, ii - jj, jj - ii)
    incl = diff >= 0
    strict = diff > 0
    eye = (ii == jj).astype(F32)

    decay = jnp.exp(jnp.where(incl, gc - gr, -jnp.inf))
    kbeta = k * beta
    k16 = k.astype(BF16)
    gram = bmm(jnp.concatenate([kbeta, q], axis=1).astype(BF16), k16, _dot_nt)
    lmat = jnp.where(strict, gram[:, :c] * decay, 0.0)
    blk = lambda s: (ii // s) == (jj // s)
    t = eye - jnp.where(blk(2), lmat, 0.0)
    s = 2
    while s < c:
        ls = jnp.where(jnp.logical_and(blk(2 * s), jnp.logical_not(blk(s))), lmat, 0.0)
        t16 = t.astype(BF16)
        t = t - bmm(bmm(t16, ls.astype(BF16)).astype(BF16), t16)
        s *= 2
    eg = jnp.exp(gc)
    rhs = jnp.concatenate([v * beta, kbeta * eg], axis=2).astype(BF16)
    uw = bmm(t.astype(BF16), rhs)
    u = uw[:, :, :DN_HEAD_DIM]
    w = uw[:, :, DN_HEAD_DIM:]
    state = s_ref[...]
    s16 = state.astype(BF16)
    ws = bmm(jnp.concatenate([w, q * eg], axis=1).astype(BF16), s16)
    v_new = u - ws[:, :c]
    qk = gram[:, c:] * decay
    v16 = v_new.astype(BF16)
    out = ws[:, c:] + bmm(qk.astype(BF16), v16)
    ke = (k * jnp.exp(gl - gc)).astype(BF16)
    s_ref[...] = state * jnp.exp(gl) + bmm(ke, v16, _dot_tn)
    o_refs = (of_ref, ob_ref)
    for m, (d, b, h) in enumerate(chains):
        o_refs[d][b, :, h * DN_HEAD_DIM:(h + 1) * DN_HEAD_DIM] = out[m].astype(BF16)


def deltanet_scan(qn, kn, vn, gb, n_lat):
    batch, t_all, _ = qn.shape
    c = DN_CHUNK
    n_chunks = t_all // c
    n_lat_chunks = n_lat // c
    n_ctx_chunks = n_chunks - n_lat_chunks

    def fwd(s):
        return jnp.where(s < n_ctx_chunks, n_lat_chunks + s, s - n_ctx_chunks)

    def bwd(s):
        return n_chunks - 1 - s

    tok_f = pl.BlockSpec((batch, c, DN_WIDTH), lambda s: (0, fwd(s), 0))
    tok_b = pl.BlockSpec((batch, c, DN_WIDTH), lambda s: (0, bwd(s), 0))
    out = jax.ShapeDtypeStruct((batch, t_all, DN_WIDTH), BF16)
    return pl.pallas_call(
        functools.partial(_dnscan_kernel, batch),
        grid=(n_chunks,),
        in_specs=[tok_f, tok_f, tok_f, pl.BlockSpec((batch, c, 128), lambda s: (0, fwd(s), 0)),
                  tok_b, tok_b, tok_b, pl.BlockSpec((batch, c, 128), lambda s: (0, bwd(s), 1))],
        out_specs=[tok_f, tok_b],
        out_shape=[out, out],
        scratch_shapes=[pltpu.VMEM((2 * batch * DN_HEADS, DN_HEAD_DIM, DN_HEAD_DIM), F32)],
        compiler_params=_params("arbitrary"),
        name="deltanet_scan",
    )(qn, kn, vn, gb, qn, kn, vn, gb)


def _attn_kernel(tq, tk, nk, lam_init, q_ref, qn_ref, k_ref, v_ref, lv_ref, gain_ref, o_ref,
                 qs_ref, qsn_ref, vp_ref, sp_ref, s0_ref, s1_ref, m_ref, acc_ref):
    dv = DA_V_DIM
    halves = (slice(0, tq), slice(tq, 2 * tq))
    first_tile = pl.program_id(2) == 0

    def stack_queries(src_ref):
        q = src_ref[0]
        lane = lax.broadcasted_iota(jnp.int32, q.shape, 1)
        zero = jnp.zeros_like(q)
        qsn_ref[0:tq, :] = jnp.where(lane < DA_QK_DIM, q, zero)
        qsn_ref[tq:2 * tq, :] = jnp.where(lane >= DA_QK_DIM, q, zero)

    def softmax_pv(j, src_ref, split):
        s = src_ref[...]
        m_old = m_ref[...]
        m_new = jnp.maximum(m_old, jnp.max(s, axis=-1, keepdims=True))
        alpha = jnp.exp2(m_old - m_new)
        p = jnp.concatenate([jnp.exp2((s[:, t:t + dv] - m_new).astype(BF16)) for t in range(0, tk, dv)], axis=1)
        alpha2 = jnp.concatenate([alpha, alpha], axis=1)
        vp = vp_ref[pl.ds(pl.multiple_of(j * tk, tk), tk), :]
        if split:
            for half in halves:
                acc_ref[half, :] = alpha2[half] * acc_ref[half, :] + _dot(p[half], vp)
        else:
            acc_ref[...] = alpha2 * acc_ref[...] + _dot(p, vp)
        m_ref[...] = m_new

    def step(j, src_ref, dst_ref):
        k_next = k_ref[0, pl.ds(pl.multiple_of((j + 1) * tk, tk), tk), :]
        dst_ref[...] = _dot_nt(qs_ref[...], k_next)
        softmax_pv(j, src_ref, False)

    @pl.when(first_tile)
    def _():
        vp_ref[:, 0:dv] = v_ref[0]
        vp_ref[:, dv:2 * dv] = jnp.ones((vp_ref.shape[0], dv), BF16)
        stack_queries(q_ref)
        for half in halves:
            sp_ref[half, :] = _dot_nt(qsn_ref[half, :], k_ref[0, 0:tk, :])

    qs_ref[...] = qsn_ref[...]
    m_ref[...] = jnp.full_like(m_ref, -jnp.inf)
    acc_ref[...] = jnp.zeros_like(acc_ref)

    if nk == 1:
        softmax_pv(0, sp_ref, True)
    else:
        ping = (s1_ref, s0_ref)
        step(0, sp_ref, s0_ref)
        n_pairs = (nk - 2) // 2
        if n_pairs > 0:
            def body(i, carry):
                step(2 * i + 1, s0_ref, s1_ref)
                step(2 * i + 2, s1_ref, s0_ref)
                return carry
            lax.fori_loop(0, n_pairs, body, 0)
        for j in range(2 * n_pairs + 1, nk - 1):
            step(j, ping[j % 2], ping[(j + 1) % 2])
        stack_queries(qn_ref)
        sp_ref[...] = _dot_nt(qsn_ref[...], k_ref[0, 0:tk, :])
        softmax_pv(nk - 1, ping[(nk - 1) % 2], False)

    lv = lv_ref[...]
    lam = (jnp.exp(jnp.sum(lv[0:1] * lv[1:2], axis=-1, keepdims=True))
           - jnp.exp(jnp.sum(lv[2:3] * lv[3:4], axis=-1, keepdims=True)) + lam_init)
    acc = acc_ref[...]
    o = acc[:, :dv] / acc[:, dv:]
    o = o[:tq] - lam * o[tq:]
    o_ref[0] = (_rms(o) * gain_ref[...] * (1.0 - lam_init)).astype(BF16)


def diff_attention(q, k, v, lam_vecs, gain, lam_init, q_rows, q_off, k_rows, k_off, tq, tk):
    batch = q.shape[0]
    nq, nk = q_rows // tq, k_rows // tk
    qo, ko = q_off // tq, k_off // k_rows
    assert nk > 1 or nq == 1
    kv = pl.BlockSpec((1, k_rows, DA_V_DIM), lambda b, h, i: (b, ko, h))
    scores = pltpu.VMEM((2 * tq, tk), F32)
    stacked_q = pltpu.VMEM((2 * tq, DA_V_DIM), BF16)
    return pl.pallas_call(
        functools.partial(_attn_kernel, tq, tk, nk, lam_init),
        grid=(batch, DA_HEADS, nq),
        in_specs=[pl.BlockSpec((1, tq, DA_V_DIM), lambda b, h, i: (b, qo + i, h)),
                  pl.BlockSpec((1, tq, DA_V_DIM), lambda b, h, i: (b, qo + jnp.minimum(i + 1, nq - 1), h)),
                  kv, kv,
                  pl.BlockSpec(lam_vecs.shape, lambda b, h, i: (0, 0)),
                  pl.BlockSpec(gain.shape, lambda b, h, i: (0, 0))],
        out_specs=pl.BlockSpec((1, tq, DA_V_DIM), lambda b, h, i: (b, i, h)),
        out_shape=jax.ShapeDtypeStruct((batch, q_rows, DA_WIDTH), BF16),
        scratch_shapes=[stacked_q, stacked_q, pltpu.VMEM((k_rows, 2 * DA_V_DIM), BF16),
                        scores, scores, scores, pltpu.VMEM((2 * tq, DA_V_DIM), F32),
                        pltpu.VMEM((2 * tq, 2 * DA_V_DIM), F32)],
        compiler_params=_params("parallel", "parallel", "arbitrary"),
        name="diff_attention",
    )(q, q, k, v, lam_vecs, gain)


def _merge_mlp_kernel(n_lat_tiles, batch, n_x, has_ctx, final, *refs):
    x_refs = refs[:n_x]
    refs = refs[n_x:]
    if has_ctx:
        (mod_ref, yf_ref, yfc_ref, odnf_ref, odnb_ref, z_ref, oda_ref, odac_ref, gate_ref, dng_ref,
         wf_ref, wdn_ref, wda_ref, wo_ref, gain2_ref, w1_ref, w2_ref, fg_ref, o_ref) = refs
    else:
        (mod_ref, yf_ref, odnf_ref, odnb_ref, z_ref, oda_ref, gate_ref, dng_ref,
         wf_ref, wdn_ref, wda_ref, wo_ref, gain2_ref, w1_ref, w2_ref, fg_ref, o_ref) = refs
    mod = _mod_row(mod_ref, n_lat_tiles, batch)
    g1 = mod[:, 2 * D_MODEL:3 * D_MODEL]
    yf = yf_ref[0]
    oda = oda_ref[0]
    if has_ctx:
        is_ctx = pl.program_id(1) >= n_lat_tiles
        yf = jnp.where(is_ctx, yfc_ref[0], yf)
        oda = jnp.where(is_ctx, odac_ref[0], oda)
    o = odnf_ref[0].astype(F32) + odnb_ref[0].astype(F32)
    z = z_ref[0].astype(F32)
    dn_parts = []
    for hd in range(DN_HEADS):
        sl = slice(hd * DN_HEAD_DIM, (hd + 1) * DN_HEAD_DIM)
        dn_parts.append(_rms(o[:, sl]) * dng_ref[...] * _silu(z[:, sl]))
    odn = jnp.concatenate(dn_parts, axis=1).astype(BF16)
    gate = gate_ref[0].astype(F32)
    merged = (gate[:, 0:D_MODEL] * _dot(yf, wf_ref[...])
              + gate[:, D_MODEL:2 * D_MODEL] * _dot(odn, wdn_ref[...])
              + gate[:, 2 * D_MODEL:] * _dot(oda, wda_ref[...]))
    y = _dot(merged.astype(BF16), wo_ref[...])
    x = _token_tile(x_refs, n_lat_tiles) + g1 * y
    sh = mod[:, 3 * D_MODEL:4 * D_MODEL]
    sc = mod[:, 4 * D_MODEL:5 * D_MODEL]
    g2 = mod[:, 5 * D_MODEL:]
    h = ((_rms(x) * gain2_ref[...]) * (1.0 + sc) + sh).astype(BF16)
    a = jnp.maximum(_dot(h, w1_ref[...]), 0.0)
    out = x + g2 * _dot((a * a).astype(BF16), w2_ref[...])
    if final:
        out = _rms(out) * fg_ref[...]
    o_ref[0] = out


def merge_and_mlp(xs, mod, yf, yf_ctx, odn, z, oda, oda_ctx, gates, dn_gain, w_f, w_dn, w_da, w_o,
                  gain2, w1, w2, final_gain, n_lat_tiles, n_tiles, final):
    batch, _, d = xs[0].shape
    tm = ROW_TILE
    has_ctx = yf_ctx is not None
    lat_last = n_lat_tiles - 1

    def rows(width):
        return pl.BlockSpec((1, tm, width), lambda b, i: (b, i, 0))

    def lat_rows(width):
        return pl.BlockSpec((1, tm, width), lambda b, i: (b, jnp.minimum(i, lat_last), 0))

    def ctx_rows(width):
        return pl.BlockSpec((1, tm, width), lambda b, i: (b, jnp.maximum(i - n_lat_tiles, 0), 0))

    args = list(xs) + [mod, yf]
    specs = _token_specs(xs, n_lat_tiles, d) + [_resident(mod.shape), lat_rows(F_WIDTH)]
    if has_ctx:
        args.append(yf_ctx)
        specs.append(ctx_rows(F_WIDTH))
    args += [odn[0], odn[1], z, oda]
    specs += [rows(DN_WIDTH), rows(DN_WIDTH), rows(DN_WIDTH), lat_rows(DA_WIDTH)]
    if has_ctx:
        args.append(oda_ctx)
        specs.append(ctx_rows(DA_WIDTH))
    weights = [dn_gain, w_f, w_dn, w_da, w_o, gain2, w1, w2, final_gain]
    args += [gates] + [_param_arg(w) for w in weights]
    specs += [rows(3 * D_MODEL)] + [_param_spec(w) for w in weights]
    return pl.pallas_call(
        functools.partial(_merge_mlp_kernel, n_lat_tiles, batch, len(xs), has_ctx, final),
        grid=(batch, n_tiles),
        in_specs=specs,
        out_specs=rows(d),
        out_shape=jax.ShapeDtypeStruct((batch, n_tiles * tm, d), F32),
        compiler_params=_params("parallel", "parallel"),
        name="merge_and_mlp",
    )(*args)


def _arrange_w_in(w_in):
    splits = np.cumsum([F_WIDTH, 3 * DN_WIDTH, DN_WIDTH, 4 * DN_HEADS, DA_WIDTH, DA_WIDTH, DA_WIDTH])
    w_uf, w_qkv, w_z, w_ab, w_q, w_k, w_v, w_g = jnp.split(w_in, splits, axis=-1)
    h = DN_HEADS
    pad = jnp.zeros(w_in.shape[:-1] + (128 - 2 * h,), w_in.dtype)
    ab = jnp.concatenate([w_ab[..., 0:h], w_ab[..., 2 * h:3 * h], pad,
                          w_ab[..., h:2 * h], w_ab[..., 3 * h:4 * h], pad], axis=-1)
    cols = [w_uf, w_qkv, w_z, w_q, w_k, w_v, w_g, ab]
    return jnp.concatenate(cols, axis=-1).astype(BF16)


def _rope_tables(n_lat, n_ctx):
    t = jnp.arange(n_lat)
    row = (t // GRID_W).astype(F32)
    col = (t % GRID_W).astype(F32)
    n_freq = DA_QK_DIM // 4
    inv_freq = ROPE_THETA ** (-jnp.arange(n_freq, dtype=F32) / n_freq)
    ang_r = row[:, None] * inv_freq[None, :]
    ang_c = col[:, None] * inv_freq[None, :]
    ang = jnp.concatenate([ang_r, ang_r, ang_c, ang_c], axis=1)
    sign = jnp.asarray(np.where((np.arange(DA_QK_DIM) & 16) == 0, -1.0, 1.0), F32)
    cos = jnp.concatenate([jnp.tile(jnp.cos(ang), (1, 2)), jnp.ones((n_ctx, 128), F32)], axis=0)
    sin = jnp.concatenate([jnp.tile(jnp.sin(ang) * sign, (1, 2)), jnp.zeros((n_ctx, 128), F32)], axis=0)
    return cos, sin


def _decay_params(a_log, dt_bias):
    h = DN_HEADS
    out = jnp.zeros((8, 256), F32)
    for d in range(2):
        out = out.at[0, d * 128 + h:d * 128 + 2 * h].set(a_log[d])
        out = out.at[1, d * 128 + h:d * 128 + 2 * h].set(dt_bias[d])
    return out


def kernel(x, c, ctx, c_ctx, norm1, norm2, w_ada, b_ada, w_in, conv_w, a_log, dt_bias, dn_gain, lam_vecs,
           da_gain, w_f, w_dn, w_da, w_o, w_mlp1, w_mlp2, final_norm):
    batch, n_lat, d = x.shape
    n_ctx = ctx.shape[1]
    depth = w_in.shape[0]
    t_all = n_lat + n_ctx
    n_lat_tiles = n_lat // ROW_TILE
    n_tiles = t_all // ROW_TILE

    cvec = jnp.zeros((8, d), F32).at[:batch].set(c).at[batch].set(c_ctx)
    mods = ada_modulation(cvec, w_ada, b_ada)
    cos_t, sin_t = _rope_tables(n_lat, n_ctx)
    xs = (x, ctx)
    final_gain = final_norm.reshape(1, d)
    w_big_all = _arrange_w_in(w_in)
    w_f16, w_dn16, w_da16, w_o16, w_mlp1_16, w_mlp2_16 = (w.astype(BF16) for w in (w_f, w_dn, w_da, w_o, w_mlp1, w_mlp2))

    for l in range(depth):
        last = l == depth - 1
        lam_init = 0.8 - 0.6 * math.exp(-0.3 * l)
        mod = mods[l]
        w_big = (w_big_all, l)
        abp = _decay_params(a_log[l], dt_bias[l])
        uf, qkv, z, q, k, v, gates, gb = input_projection(
            xs, mod, norm1[l].reshape(1, d), w_big, cos_t, sin_t, abp, n_lat_tiles)

        yf = fourier_latent(uf, n_lat)
        conv_w8 = jnp.zeros((8, 3 * DN_WIDTH), F32).at[:DN_CONV].set(conv_w[l])
        qn, kn, vn = deltanet_prep(qkv, conv_w8, n_lat_tiles)
        odn = deltanet_scan(qn, kn, vn, gb, n_lat)
        gain_da = da_gain[l].reshape(1, DA_V_DIM)
        tk = next(t for t in (768, 512, 256) if t_all % t == 0)
        tq = min(ATTN_Q_TILE, n_lat)
        oda = diff_attention(q, k, v, lam_vecs[l], gain_da, lam_init, n_lat, 0, t_all, 0, tq, tk)
        if last:
            yf_ctx = oda_ctx = None
            tiles = n_lat_tiles
        else:
            yf_ctx = fourier_context(uf, n_lat, n_ctx)
            oda_ctx = diff_attention(q, k, v, lam_vecs[l], gain_da, lam_init, n_ctx, n_lat, n_ctx, n_lat,
                                     n_ctx, n_ctx)
            tiles = n_tiles
        out = merge_and_mlp(xs, mod, yf, yf_ctx, odn, z, oda, oda_ctx, gates, dn_gain[l].reshape(1, DN_HEAD_DIM),
                            (w_f16, l), (w_dn16, l), (w_da16, l), (w_o16, l),
                            norm2[l].reshape(1, d), (w_mlp1_16, l), (w_mlp2_16, l), final_gain,
                            n_lat_tiles, tiles, last)
        xs = (out,)
    return xs[0]
```

```python
import functools
import math

import numpy as np
import jax
import jax.numpy as jnp
from jax import lax
from jax.experimental import pallas as pl
from jax.experimental.pallas import tpu as pltpu

F32 = jnp.float32
BF16 = jnp.bfloat16

D_MODEL = 1024
GRID_W = 64
F_GROUPS = 4
F_GROUP_DIM = 128
F_WIDTH = F_GROUPS * F_GROUP_DIM
DN_HEADS = 4
DN_HEAD_DIM = 128
DN_WIDTH = DN_HEADS * DN_HEAD_DIM
DN_CONV = 5
DN_CHUNK = 64
DA_HEADS = 4
DA_QK_DIM = 64
DA_V_DIM = 2 * DA_QK_DIM
DA_WIDTH = DA_HEADS * DA_V_DIM
ROPE_THETA = 10000.0
D_FF = 4 * D_MODEL
NORM_EPS = 1e-6
LOG2_E = math.log2(math.e)

ROW_TILE = 256
ATTN_STEPS_PER_BODY = 4
ATTN_Q_TILE = 1024
CONV_HALO = 16
VMEM_LIMIT = 56 * 1024 * 1024

C_UF = (0, 0)
C_QKV = (0, F_WIDTH)
C_Z = (0, F_WIDTH + 3 * DN_WIDTH)
C_Q = (1, 0)
C_K = (1, DA_WIDTH)
C_V = (1, 2 * DA_WIDTH)
C_GATE = (1, 3 * DA_WIDTH)
C_AB = (2, 0)


def _dot(a, b):
    return jnp.dot(a, b, preferred_element_type=F32)


def _dot_f32(a, b):
    return jnp.dot(a, b, preferred_element_type=F32, precision=lax.Precision.HIGHEST)


def _dot_nt(a, b):
    return lax.dot_general(a, b, (((1,), (1,)), ((), ())), preferred_element_type=F32)


def _dot_tn(a, b):
    return lax.dot_general(a, b, (((0,), (0,)), ((), ())), preferred_element_type=F32)


def _silu(x):
    return x * jax.nn.sigmoid(x)


def _softplus(x):
    return jnp.maximum(x, 0.0) + jnp.log1p(jnp.exp(-jnp.abs(x)))


def _params(*sem):
    return pltpu.CompilerParams(dimension_semantics=sem, vmem_limit_bytes=VMEM_LIMIT)


def _resident(shape):
    nd = len(shape)
    return pl.BlockSpec(shape, lambda *_: (0,) * nd, pipeline_mode=pl.Buffered(1))


def _param_spec(p):
    if isinstance(p, tuple):
        arr, layer = p
        return pl.BlockSpec((None,) + arr.shape[1:], lambda *_: (layer,) + (0,) * (arr.ndim - 1),
                            pipeline_mode=pl.Buffered(1))
    return _resident(p.shape)


def _param_arg(p):
    return p[0] if isinstance(p, tuple) else p


def _ada_kernel(c_ref, w_ref, b_ref, o_ref):
    a = _silu(c_ref[...])
    o_ref[0] = _dot_f32(a, w_ref[0]) + b_ref[0]


def ada_modulation(cvec, w_ada, b_ada):
    depth, d, n = w_ada.shape
    tn = 1536
    return pl.pallas_call(
        _ada_kernel,
        grid=(depth, n // tn),
        in_specs=[pl.BlockSpec((8, d), lambda l, j: (0, 0)),
                  pl.BlockSpec((1, d, tn), lambda l, j: (l, 0, j)),
                  pl.BlockSpec((1, 1, tn), lambda l, j: (l, 0, j))],
        out_specs=pl.BlockSpec((1, 8, tn), lambda l, j: (l, 0, j)),
        out_shape=jax.ShapeDtypeStruct((depth, 8, n), F32),
        compiler_params=_params("arbitrary", "arbitrary"),
        name="ada_modulation",
    )(cvec, w_ada, b_ada.reshape(depth, 1, n))


def _mod_row(mod_ref, n_lat_tiles, batch):
    i = pl.program_id(1)
    b = pl.program_id(0)
    r = jnp.where(i >= n_lat_tiles, batch, b)
    return mod_ref[pl.ds(r, 1), :]


def _rms(x):
    return x * lax.rsqrt(jnp.mean(x * x, axis=-1, keepdims=True) + NORM_EPS)


def _token_specs(arrays, n_lat_tiles, width):
    tm = ROW_TILE
    if len(arrays) == 1:
        return [pl.BlockSpec((1, tm, width), lambda b, i: (b, i, 0))]
    return [pl.BlockSpec((1, tm, width), lambda b, i: (b, jnp.minimum(i, n_lat_tiles - 1), 0)),
            pl.BlockSpec((1, tm, width), lambda b, i: (b, jnp.maximum(i - n_lat_tiles, 0), 0))]


def _token_tile(refs, n_lat_tiles):
    if len(refs) == 1:
        return refs[0][0]
    return jnp.where(pl.program_id(1) >= n_lat_tiles, refs[1][0], refs[0][0])


def _rotary(x, cos, sin_signed):
    width = x.shape[-1]
    lane = lax.broadcasted_iota(jnp.int32, x.shape, 1)
    partner = jnp.where((lane & 16) == 0, pltpu.roll(x, width - 16, 1), pltpu.roll(x, 16, 1))
    return x * cos + partner * sin_signed


def _inproj_kernel(n_lat_tiles, batch, n_x, *refs):
    x_refs = refs[:n_x]
    (mod_ref, gain_ref, wa_ref, wb_ref, wc_ref, cos_ref, sin_ref, abp_ref,
     uf_ref, qkv_ref, z_ref, q_ref, k_ref, v_ref, gate_ref, gb_ref) = refs[n_x:]
    w_refs = (wa_ref, wb_ref, wc_ref)
    mod = _mod_row(mod_ref, n_lat_tiles, batch)
    sh = mod[:, 0:D_MODEL]
    sc = mod[:, D_MODEL:2 * D_MODEL]
    h = (_rms(_token_tile(x_refs, n_lat_tiles)) * gain_ref[...]) * (1.0 + sc) + sh
    hb = h.astype(BF16)

    def proj(group, width, skip=0):
        part, c0 = group
        return _dot(hb, w_refs[part][:, c0 + skip:c0 + skip + width])

    ab = proj(C_AB, 256)
    abp = abp_ref[...]
    lane = lax.broadcasted_iota(jnp.int32, ab.shape, 1) % 128
    beta = jax.nn.sigmoid(ab)
    g = -jnp.exp(abp[0:1, :]) * _softplus(ab + abp[1:2, :])
    gb_ref[0] = jnp.where(lane < DN_HEADS, beta, g)
    cos = jnp.tile(cos_ref[...], (1, DA_HEADS))
    sin = jnp.tile(sin_ref[...], (1, DA_HEADS))
    q = _rotary(proj(C_Q, DA_WIDTH), cos, sin)
    q_ref[0] = (q * (DA_QK_DIM ** -0.5 * LOG2_E)).astype(BF16)
    k_ref[0] = _rotary(proj(C_K, DA_WIDTH), cos, sin).astype(BF16)
    for j in range(3 * D_MODEL // 512):
        g = proj(C_GATE, 512, j * 512)
        gate_ref[0, :, j * 512:(j + 1) * 512] = jax.nn.sigmoid(g).astype(BF16)
    uf_ref[0] = proj(C_UF, F_WIDTH).astype(BF16)
    z_ref[0] = proj(C_Z, DN_WIDTH).astype(BF16)
    v_ref[0] = proj(C_V, DA_WIDTH).astype(BF16)
    for j in range(3):
        qkv_ref[0, :, j * DN_WIDTH:(j + 1) * DN_WIDTH] = proj(C_QKV, DN_WIDTH, j * DN_WIDTH).astype(BF16)


def input_projection(xs, mod, gain, w_big, cos_t, sin_t, abp, n_lat_tiles):
    batch, _, d = xs[0].shape
    t_all = sum(a.shape[1] for a in xs)
    n_tiles = t_all // ROW_TILE
    tm = ROW_TILE

    def rows(width, dtype):
        return (pl.BlockSpec((1, tm, width), lambda b, i: (b, i, 0)),
                jax.ShapeDtypeStruct((batch, t_all, width), dtype))

    outs = [rows(F_WIDTH, BF16), rows(3 * DN_WIDTH, BF16), rows(DN_WIDTH, BF16), rows(DA_WIDTH, BF16),
            rows(DA_WIDTH, BF16), rows(DA_WIDTH, BF16), rows(3 * D_MODEL, BF16), rows(256, F32)]
    return pl.pallas_call(
        functools.partial(_inproj_kernel, n_lat_tiles, batch, len(xs)),
        grid=(batch, n_tiles),
        in_specs=_token_specs(xs, n_lat_tiles, d) + [
            _resident(mod.shape), _resident(gain.shape)] + [_param_spec(w) for w in w_big] + [
            pl.BlockSpec((tm, 128), lambda b, i: (i, 0)),
            pl.BlockSpec((tm, 128), lambda b, i: (i, 0)),
            _resident(abp.shape)],
        out_specs=[o[0] for o in outs],
        out_shape=[o[1] for o in outs],
        compiler_params=_params("parallel", "parallel"),
        name="input_projection",
    )(*xs, mod, gain, *[_param_arg(w) for w in w_big], cos_t, sin_t, abp)


def _dft_tables(n_pos):
    n1 = n_pos // GRID_W
    n2 = GRID_W
    a = np.arange(n1)
    ang1 = 2.0 * np.pi * ((a[:, None] * a[None, :]) % n1) / n1
    stage1 = np.concatenate([np.cos(ang1), -np.sin(ang1)], axis=0)
    k1 = np.arange(n1)[:, None, None]
    k2 = np.arange(n2)[None, :, None]
    m = np.arange(n2)[None, None, :]
    ang2 = 2.0 * np.pi * ((m * (k1 + n1 * k2)) % n_pos) / n_pos
    c2, s2 = np.cos(ang2), np.sin(ang2)
    stage2 = np.concatenate([np.concatenate([c2, s2], axis=2),
                             np.concatenate([-s2, c2], axis=2)], axis=1)
    return stage1, stage2


def _channel_table(n_pos):
    c = np.arange(F_GROUP_DIM)
    ang = 2.0 * np.pi * ((c[:, None] * c[None, :]) % F_GROUP_DIM) / F_GROUP_DIM
    scale = 1.0 / math.sqrt(n_pos * F_GROUP_DIM)
    return np.concatenate([np.cos(ang), np.sin(ang)], axis=0) * scale


def _channel_mix(pr, pi, chan):
    outs = []
    for g in range(F_GROUPS):
        sl = slice(g * F_GROUP_DIM, (g + 1) * F_GROUP_DIM)
        pg = jnp.concatenate([pr[:, sl], pi[:, sl]], axis=1).astype(BF16)
        outs.append(_dot(pg, chan))
    return jnp.concatenate(outs, axis=1)


def _fft1_kernel(n1, f_ref, x_ref, ar_ref, ai_ref):
    a = _dot(f_ref[...].astype(BF16), x_ref[0])
    ar_ref[0] = a[:n1].astype(BF16)
    ai_ref[0] = a[n1:].astype(BF16)


def _fft2_kernel(kb, m_ref, ar_ref, ai_ref, chan_ref, o_ref):
    chan = chan_ref[...].astype(BF16)
    a = [jnp.concatenate([ar_ref[0, j], ai_ref[0, j]], axis=0) for j in range(kb)]
    p = [_dot(m_ref[j].astype(BF16), a[j]) for j in range(kb)]
    pg = [[jnp.concatenate([p[j][:GRID_W, g * F_GROUP_DIM:(g + 1) * F_GROUP_DIM],
                            p[j][GRID_W:, g * F_GROUP_DIM:(g + 1) * F_GROUP_DIM]], axis=1).astype(BF16)
           for g in range(F_GROUPS)] for j in range(kb)]
    y = [[_dot(pg[j][g], chan) for g in range(F_GROUPS)] for j in range(kb)]
    for j in range(kb):
        o_ref[0, :, j * F_WIDTH:(j + 1) * F_WIDTH] = jnp.concatenate(y[j], axis=1).astype(BF16)


def fourier_latent(uf_all, n_pos):
    batch, t_all, _ = uf_all.shape
    n1 = n_pos // GRID_W
    stage1, stage2 = _dft_tables(n_pos)
    f1 = jnp.asarray(stage1, F32)
    m2 = jnp.asarray(stage2, F32)
    chan = jnp.asarray(_channel_table(n_pos), F32)
    wide = GRID_W * F_WIDTH
    x2 = uf_all.reshape(batch, t_all // GRID_W, wide)
    tn = 4096
    ar, ai = pl.pallas_call(
        functools.partial(_fft1_kernel, n1),
        grid=(batch, wide // tn),
        in_specs=[_resident(f1.shape),
                  pl.BlockSpec((1, n1, tn), lambda b, j: (b, 0, j))],
        out_specs=[pl.BlockSpec((1, n1, tn), lambda b, j: (b, 0, j))] * 2,
        out_shape=[jax.ShapeDtypeStruct((batch, n1, wide), BF16)] * 2,
        compiler_params=_params("parallel", "parallel"),
        name="fourier_stage1",
    )(f1, x2)
    ar = ar.reshape(batch, n1, GRID_W, F_WIDTH)
    ai = ai.reshape(batch, n1, GRID_W, F_WIDTH)
    kb = 16 if n1 % 16 == 0 else 8
    y = pl.pallas_call(
        functools.partial(_fft2_kernel, kb),
        grid=(batch, n1 // kb),
        in_specs=[pl.BlockSpec((kb, 2 * GRID_W, 2 * GRID_W), lambda b, i: (i, 0, 0)),
                  pl.BlockSpec((1, kb, GRID_W, F_WIDTH), lambda b, i: (b, i, 0, 0)),
                  pl.BlockSpec((1, kb, GRID_W, F_WIDTH), lambda b, i: (b, i, 0, 0)),
                  _resident(chan.shape)],
        out_specs=pl.BlockSpec((1, GRID_W, kb * F_WIDTH), lambda b, i: (b, 0, i)),
        out_shape=jax.ShapeDtypeStruct((batch, GRID_W, n1 * F_WIDTH), BF16),
        compiler_params=_params("parallel", "parallel"),
        name="fourier_stage2",
    )(m2, ar, ai, chan)
    return y.reshape(batch, n_pos, F_WIDTH)


def _fft_ctx_kernel(tc, f_ref, x_ref, chan_ref, o_ref):
    p = _dot(f_ref[...].astype(BF16), x_ref[0])
    o_ref[0] = _channel_mix(p[:tc], p[tc:], chan_ref[...].astype(BF16)).astype(BF16)


def fourier_context(uf_all, n_pos, tc):
    batch = uf_all.shape[0]
    a = np.arange(tc)
    ang = 2.0 * np.pi * ((a[:, None] * a[None, :]) % tc) / tc
    f = jnp.asarray(np.concatenate([np.cos(ang), -np.sin(ang)], axis=0), F32)
    chan = jnp.asarray(_channel_table(tc), F32)
    return pl.pallas_call(
        functools.partial(_fft_ctx_kernel, tc),
        grid=(batch,),
        in_specs=[_resident(f.shape),
                  pl.BlockSpec((1, tc, F_WIDTH), lambda b: (b, n_pos // tc, 0)),
                  _resident(chan.shape)],
        out_specs=pl.BlockSpec((1, tc, F_WIDTH), lambda b: (b, 0, 0)),
        out_shape=jax.ShapeDtypeStruct((batch, tc, F_WIDTH), BF16),
        compiler_params=_params("parallel"),
        name="fourier_context",
    )(f, uf_all, chan)


def _dnprep_kernel(n_lat_tiles, n_tiles, cur_ref, prev_ref, next_ref, w_ref, q_ref, k_ref, v_ref):
    i = pl.program_id(1)
    tm = ROW_TILE
    h = CONV_HALO
    prev_ok = jnp.logical_and(i != 0, i != n_lat_tiles)
    next_ok = jnp.logical_and(i != n_lat_tiles - 1, i != n_tiles - 1)
    pad = DN_CONV // 2
    t_out = lax.broadcasted_iota(jnp.int32, (tm, tm), 0)
    t_in = lax.broadcasted_iota(jnp.int32, (tm, tm), 1)
    r_out = lax.broadcasted_iota(jnp.int32, (8, h), 0)
    r_in = lax.broadcasted_iota(jnp.int32, (8, h), 1)
    shift = {j: (t_in == t_out + (j - pad)).astype(BF16) for j in range(DN_CONV) if j != pad}
    from_prev = {j: (r_in == r_out + (h + j - pad)).astype(BF16) for j in range(pad)}
    from_next = {j: (r_in == r_out + (j - pad - 8)).astype(BF16) for j in range(pad + 1, DN_CONV)}

    def l2n(block):
        return block * lax.rsqrt(jnp.sum(block * block, axis=-1, keepdims=True) + NORM_EPS)

    block_w = 2 * DN_HEAD_DIM
    for c0 in range(0, 3 * DN_WIDTH, block_w):
        cols = slice(c0, c0 + block_w)
        cur = cur_ref[0, :, cols]
        prev = jnp.where(prev_ok, prev_ref[0, :, cols], jnp.zeros((h, block_w), BF16))
        nxt = jnp.where(next_ok, next_ref[0, :, cols], jnp.zeros((h, block_w), BF16))
        acc = cur.astype(F32) * w_ref[pad:pad + 1, cols]
        for j, s_mat in shift.items():
            acc = acc + _dot(s_mat, cur) * w_ref[j:j + 1, cols]
        top = sum(_dot(m, prev) * w_ref[j:j + 1, cols] for j, m in from_prev.items())
        bot = sum(_dot(m, nxt) * w_ref[j:j + 1, cols] for j, m in from_next.items())
        acc = jnp.concatenate([acc[:8] + top, acc[8:tm - 8], acc[tm - 8:] + bot], axis=0)
        u = _silu(acc)
        for half in range(2):
            blk = u[:, half * DN_HEAD_DIM:(half + 1) * DN_HEAD_DIM]
            c = c0 + half * DN_HEAD_DIM
            if c < DN_WIDTH:
                q_ref[0, :, c:c + DN_HEAD_DIM] = (l2n(blk) * (DN_HEAD_DIM ** -0.5)).astype(BF16)
            elif c < 2 * DN_WIDTH:
                k_ref[0, :, c - DN_WIDTH:c - DN_WIDTH + DN_HEAD_DIM] = l2n(blk).astype(BF16)
            else:
                v_ref[0, :, c - 2 * DN_WIDTH:c - 2 * DN_WIDTH + DN_HEAD_DIM] = blk.astype(BF16)


def deltanet_prep(qkv, conv_w8, n_lat_tiles):
    batch, t_all, width = qkv.shape
    tm = ROW_TILE
    n_tiles = t_all // tm
    per = tm // CONV_HALO
    last = t_all // CONV_HALO - 1
    out = (pl.BlockSpec((1, tm, DN_WIDTH), lambda b, i: (b, i, 0)),
           jax.ShapeDtypeStruct((batch, t_all, DN_WIDTH), BF16))
    return pl.pallas_call(
        functools.partial(_dnprep_kernel, n_lat_tiles, n_tiles),
        grid=(batch, n_tiles),
        in_specs=[pl.BlockSpec((1, tm, width), lambda b, i: (b, i, 0)),
                  pl.BlockSpec((1, CONV_HALO, width), lambda b, i: (b, jnp.maximum(i * per - 1, 0), 0)),
                  pl.BlockSpec((1, CONV_HALO, width), lambda b, i: (b, jnp.minimum((i + 1) * per, last), 0)),
                  _resident(conv_w8.shape)],
        out_specs=[out[0]] * 3,
        out_shape=[out[1]] * 3,
        compiler_params=_params("parallel", "parallel"),
        name="deltanet_prep",
    )(qkv, qkv, qkv, conv_w8)


def _dnscan_kernel(batch, qf_ref, kf_ref, vf_ref, gf_ref, qb_ref, kb_ref, vb_ref, gb_ref, of_ref, ob_ref, s_ref):
    c = DN_CHUNK
    hh = DN_HEADS
    chains = [(d, b, h) for d in range(2) for b in range(batch) for h in range(hh)]
    n = len(chains)
    half = n // 2

    @pl.when(pl.program_id(0) == 0)
    def _():
        s_ref[...] = jnp.zeros_like(s_ref)

    row = lax.broadcasted_iota(jnp.int32, (c, 128), 0)

    def prefix_sum(g, d):
        x = g
        shift = 1
        while shift < c:
            if d == 0:
                x = x + jnp.where(row >= shift, pltpu.roll(x, shift, 0), 0.0)
            else:
                x = x + jnp.where(row < c - shift, pltpu.roll(x, c - shift, 0), 0.0)
            shift *= 2
        return x

    g_refs = (gf_ref, gb_ref)
    gcum, grow, gtot, graw = {}, {}, {}, {}
    for d in range(2):
        for b in range(batch):
            g = g_refs[d][b]
            graw[d, b] = g
            gcum[d, b] = prefix_sum(g, d)
            grow[d, b] = gcum[d, b].T
            gtot[d, b] = jnp.sum(g, axis=0, keepdims=True)

    def per_chain(fn):
        return jnp.stack([fn(d, b, h) for d, b, h in chains])

    def tile(refs):
        return per_chain(lambda d, b, h: refs[d][b, :, h * DN_HEAD_DIM:(h + 1) * DN_HEAD_DIM]).astype(F32)

    q = tile((qf_ref, qb_ref))
    k = tile((kf_ref, kb_ref))
    v = tile((vf_ref, vb_ref))
    beta = per_chain(lambda d, b, h: graw[d, b][:, h:h + 1])
    gc = per_chain(lambda d, b, h: gcum[d, b][:, hh + h:hh + h + 1])
    gr = per_chain(lambda d, b, h: grow[d, b][hh + h:hh + h + 1, :])
    gl = per_chain(lambda d, b, h: gtot[d, b][:, hh + h:hh + h + 1])

    def bmm(a, bm, fn=_dot):
        return jnp.stack([fn(a[m], bm[m]) for m in range(n)])

    ch = lax.broadcasted_iota(jnp.int32, (n, c, c), 0)
    ii = lax.broadcasted_iota(jnp.int32, (n, c, c), 1)
    jj = lax.broadcasted_iota(jnp.int32, (n, c, c), 2)
    diff = jnp.where(ch < half, ii - jj, jj - ii)
    incl = diff >= 0
    strict = diff > 0
    eye = (ii == jj).astype(F32)

    decay = jnp.exp(jnp.where(incl, gc - gr, -jnp.inf))
    kbeta = k * beta
    k16 = k.astype(BF16)
    gram = bmm(jnp.concatenate([kbeta, q], axis=1).astype(BF16), k16, _dot_nt)
    lmat = jnp.where(strict, gram[:, :c] * decay, 0.0)
    blk = lambda s: (ii // s) == (jj // s)
    t = eye - jnp.where(blk(2), lmat, 0.0)
    s = 2
    while s < c:
        ls = jnp.where(jnp.logical_and(blk(2 * s), jnp.logical_not(blk(s))), lmat, 0.0)
        t16 = t.astype(BF16)
        t = t - bmm(bmm(t16, ls.astype(BF16)).astype(BF16), t16)
        s *= 2
    eg = jnp.exp(gc)
    rhs = jnp.concatenate([v * beta, kbeta * eg], axis=2).astype(BF16)
    uw = bmm(t.astype(BF16), rhs)
    u = uw[:, :, :DN_HEAD_DIM]
    w = uw[:, :, DN_HEAD_DIM:]
    state = s_ref[...]
    s16 = state.astype(BF16)
    ws = bmm(jnp.concatenate([w, q * eg], axis=1).astype(BF16), s16)
    v_new = u - ws[:, :c]
    qk = gram[:, c:] * decay
    v16 = v_new.astype(BF16)
    out = ws[:, c:] + bmm(qk.astype(BF16), v16)
    ke = (k * jnp.exp(gl - gc)).astype(BF16)
    s_ref[...] = state * jnp.exp(gl) + bmm(ke, v16, _dot_tn)
    o_refs = (of_ref, ob_ref)
    for m, (d, b, h) in enumerate(chains):
        o_refs[d][b, :, h * DN_HEAD_DIM:(h + 1) * DN_HEAD_DIM] = out[m].astype(BF16)


def deltanet_scan(qn, kn, vn, gb, n_lat):
    batch, t_all, _ = qn.shape
    c = DN_CHUNK
    n_chunks = t_all // c
    n_lat_chunks = n_lat // c
    n_ctx_chunks = n_chunks - n_lat_chunks

    def fwd(s):
        return jnp.where(s < n_ctx_chunks, n_lat_chunks + s, s - n_ctx_chunks)

    def bwd(s):
        return n_chunks - 1 - s

    tok_f = pl.BlockSpec((batch, c, DN_WIDTH), lambda s: (0, fwd(s), 0))
    tok_b = pl.BlockSpec((batch, c, DN_WIDTH), lambda s: (0, bwd(s), 0))
    out = jax.ShapeDtypeStruct((batch, t_all, DN_WIDTH), BF16)
    return pl.pallas_call(
        functools.partial(_dnscan_kernel, batch),
        grid=(n_chunks,),
        in_specs=[tok_f, tok_f, tok_f, pl.BlockSpec((batch, c, 128), lambda s: (0, fwd(s), 0)),
                  tok_b, tok_b, tok_b, pl.BlockSpec((batch, c, 128), lambda s: (0, bwd(s), 1))],
        out_specs=[tok_f, tok_b],
        out_shape=[out, out],
        scratch_shapes=[pltpu.VMEM((2 * batch * DN_HEADS, DN_HEAD_DIM, DN_HEAD_DIM), F32)],
        compiler_params=_params("arbitrary"),
        name="deltanet_scan",
    )(qn, kn, vn, gb, qn, kn, vn, gb)


def _attn_kernel(tq, tk, nk, lam_init, q_ref, qn_ref, k_ref, v_ref, lv_ref, gain_ref, o_ref,
                 qs_ref, qsn_ref, vp_ref, sp_ref, s0_ref, s1_ref, m_ref, acc_ref):
    dv = DA_V_DIM
    halves = (slice(0, tq), slice(tq, 2 * tq))
    first_tile = pl.program_id(2) == 0

    def stack_queries(src_ref):
        q = src_ref[0]
        lane = lax.broadcasted_iota(jnp.int32, q.shape, 1)
        zero = jnp.zeros_like(q)
        qsn_ref[0:tq, :] = jnp.where(lane < DA_QK_DIM, q, zero)
        qsn_ref[tq:2 * tq, :] = jnp.where(lane >= DA_QK_DIM, q, zero)

    def softmax_pv(j, src_ref, split):
        s = src_ref[...]
        m_old = m_ref[...]
        m_new = jnp.maximum(m_old, jnp.max(s, axis=-1, keepdims=True))
        alpha = jnp.exp2(m_old - m_new)
        p = jnp.concatenate([jnp.exp2((s[:, t:t + dv] - m_new).astype(BF16)) for t in range(0, tk, dv)], axis=1)
        alpha2 = jnp.concatenate([alpha, alpha], axis=1)
        vp = vp_ref[pl.ds(pl.multiple_of(j * tk, tk), tk), :]
        if split:
            for half in halves:
                acc_ref[half, :] = alpha2[half] * acc_ref[half, :] + _dot(p[half], vp)
        else:
            acc_ref[...] = alpha2 * acc_ref[...] + _dot(p, vp)
        m_ref[...] = m_new

    def step(j, src_ref, dst_ref):
        k_next = k_ref[0, pl.ds(pl.multiple_of((j + 1) * tk, tk), tk), :]
        dst_ref[...] = _dot_nt(qs_ref[...], k_next)
        softmax_pv(j, src_ref, False)

    @pl.when(first_tile)
    def _():
        vp_ref[:, 0:dv] = v_ref[0]
        vp_ref[:, dv:2 * dv] = jnp.ones((vp_ref.shape[0], dv), BF16)
        stack_queries(q_ref)
        for half in halves:
            sp_ref[half, :] = _dot_nt(qsn_ref[half, :], k_ref[0, 0:tk, :])

    qs_ref[...] = qsn_ref[...]
    m_ref[...] = jnp.full_like(m_ref, -jnp.inf)
    acc_ref[...] = jnp.zeros_like(acc_ref)

    if nk == 1:
        softmax_pv(0, sp_ref, True)
    else:
        ping = (s1_ref, s0_ref)
        step(0, sp_ref, s0_ref)
        per_body = ATTN_STEPS_PER_BODY
        n_bodies = (nk - 2) // per_body
        if n_bodies > 0:
            def body(i, carry):
                for r in range(per_body):
                    j = per_body * i + 1 + r
                    step(j, ping[(r + 1) % 2], ping[r % 2])
                return carry
            lax.fori_loop(0, n_bodies, body, 0)
        for j in range(per_body * n_bodies + 1, nk - 1):
            step(j, ping[j % 2], ping[(j + 1) % 2])
        stack_queries(qn_ref)
        sp_ref[...] = _dot_nt(qsn_ref[...], k_ref[0, 0:tk, :])
        softmax_pv(nk - 1, ping[(nk - 1) % 2], False)

    lv = lv_ref[...]
    lam = (jnp.exp(jnp.sum(lv[0:1] * lv[1:2], axis=-1, keepdims=True))
           - jnp.exp(jnp.sum(lv[2:3] * lv[3:4], axis=-1, keepdims=True)) + lam_init)
    acc = acc_ref[...]
    o = acc[:, :dv] / acc[:, dv:]
    o = o[:tq] - lam * o[tq:]
    o_ref[0] = (_rms(o) * gain_ref[...] * (1.0 - lam_init)).astype(BF16)


def diff_attention(q, k, v, lam_vecs, gain, lam_init, q_rows, q_off, k_rows, k_off, tq, tk):
    batch = q.shape[0]
    nq, nk = q_rows // tq, k_rows // tk
    qo, ko = q_off // tq, k_off // k_rows
    assert nk > 1 or nq == 1
    kv = pl.BlockSpec((1, k_rows, DA_V_DIM), lambda b, h, i: (b, ko, h))
    scores = pltpu.VMEM((2 * tq, tk), F32)
    stacked_q = pltpu.VMEM((2 * tq, DA_V_DIM), BF16)
    return pl.pallas_call(
        functools.partial(_attn_kernel, tq, tk, nk, lam_init),
        grid=(batch, DA_HEADS, nq),
        in_specs=[pl.BlockSpec((1, tq, DA_V_DIM), lambda b, h, i: (b, qo + i, h)),
                  pl.BlockSpec((1, tq, DA_V_DIM), lambda b, h, i: (b, qo + jnp.minimum(i + 1, nq - 1), h)),
                  kv, kv,
                  pl.BlockSpec(lam_vecs.shape, lambda b, h, i: (0, 0)),
                  pl.BlockSpec(gain.shape, lambda b, h, i: (0, 0))],
        out_specs=pl.BlockSpec((1, tq, DA_V_DIM), lambda b, h, i: (b, i, h)),
        out_shape=jax.ShapeDtypeStruct((batch, q_rows, DA_WIDTH), BF16),
        scratch_shapes=[stacked_q, stacked_q, pltpu.VMEM((k_rows, 2 * DA_V_DIM), BF16),
                        scores, scores, scores, pltpu.VMEM((2 * tq, DA_V_DIM), F32),
                        pltpu.VMEM((2 * tq, 2 * DA_V_DIM), F32)],
        compiler_params=_params("parallel", "parallel", "arbitrary"),
        name="diff_attention",
    )(q, q, k, v, lam_vecs, gain)


def _merge_mlp_kernel(n_lat_tiles, batch, n_x, has_ctx, final, *refs):
    x_refs = refs[:n_x]
    refs = refs[n_x:]
    if has_ctx:
        (mod_ref, yf_ref, yfc_ref, odnf_ref, odnb_ref, z_ref, oda_ref, odac_ref, gate_ref, dng_ref,
         wf_ref, wdn_ref, wda_ref, wo_ref, gain2_ref, w1_ref, w2_ref, fg_ref, o_ref) = refs
    else:
        (mod_ref, yf_ref, odnf_ref, odnb_ref, z_ref, oda_ref, gate_ref, dng_ref,
         wf_ref, wdn_ref, wda_ref, wo_ref, gain2_ref, w1_ref, w2_ref, fg_ref, o_ref) = refs
    mod = _mod_row(mod_ref, n_lat_tiles, batch)
    g1 = mod[:, 2 * D_MODEL:3 * D_MODEL]
    yf = yf_ref[0]
    oda = oda_ref[0]
    if has_ctx:
        is_ctx = pl.program_id(1) >= n_lat_tiles
        yf = jnp.where(is_ctx, yfc_ref[0], yf)
        oda = jnp.where(is_ctx, odac_ref[0], oda)
    o = odnf_ref[0].astype(F32) + odnb_ref[0].astype(F32)
    z = z_ref[0].astype(F32)
    dn_parts = []
    for hd in range(DN_HEADS):
        sl = slice(hd * DN_HEAD_DIM, (hd + 1) * DN_HEAD_DIM)
        dn_parts.append(_rms(o[:, sl]) * dng_ref[...] * _silu(z[:, sl]))
    odn = jnp.concatenate(dn_parts, axis=1).astype(BF16)
    gate = gate_ref[0].astype(F32)
    merged = (gate[:, 0:D_MODEL] * _dot(yf, wf_ref[...])
              + gate[:, D_MODEL:2 * D_MODEL] * _dot(odn, wdn_ref[...])
              + gate[:, 2 * D_MODEL:] * _dot(oda, wda_ref[...]))
    y = _dot(merged.astype(BF16), wo_ref[...])
    x = _token_tile(x_refs, n_lat_tiles) + g1 * y
    sh = mod[:, 3 * D_MODEL:4 * D_MODEL]
    sc = mod[:, 4 * D_MODEL:5 * D_MODEL]
    g2 = mod[:, 5 * D_MODEL:]
    h = ((_rms(x) * gain2_ref[...]) * (1.0 + sc) + sh).astype(BF16)
    a = jnp.maximum(_dot(h, w1_ref[...]), 0.0)
    out = x + g2 * _dot((a * a).astype(BF16), w2_ref[...])
    if final:
        out = _rms(out) * fg_ref[...]
    o_ref[0] = out


def merge_and_mlp(xs, mod, yf, yf_ctx, odn, z, oda, oda_ctx, gates, dn_gain, w_f, w_dn, w_da, w_o,
                  gain2, w1, w2, final_gain, n_lat_tiles, n_tiles, final):
    batch, _, d = xs[0].shape
    tm = ROW_TILE
    has_ctx = yf_ctx is not None
    lat_last = n_lat_tiles - 1

    def rows(width):
        return pl.BlockSpec((1, tm, width), lambda b, i: (b, i, 0))

    def lat_rows(width):
        return pl.BlockSpec((1, tm, width), lambda b, i: (b, jnp.minimum(i, lat_last), 0))

    def ctx_rows(width):
        return pl.BlockSpec((1, tm, width), lambda b, i: (b, jnp.maximum(i - n_lat_tiles, 0), 0))

    args = list(xs) + [mod, yf]
    specs = _token_specs(xs, n_lat_tiles, d) + [_resident(mod.shape), lat_rows(F_WIDTH)]
    if has_ctx:
        args.append(yf_ctx)
        specs.append(ctx_rows(F_WIDTH))
    args += [odn[0], odn[1], z, oda]
    specs += [rows(DN_WIDTH), rows(DN_WIDTH), rows(DN_WIDTH), lat_rows(DA_WIDTH)]
    if has_ctx:
        args.append(oda_ctx)
        specs.append(ctx_rows(DA_WIDTH))
    weights = [dn_gain, w_f, w_dn, w_da, w_o, gain2, w1, w2, final_gain]
    args += [gates] + [_param_arg(w) for w in weights]
    specs += [rows(3 * D_MODEL)] + [_param_spec(w) for w in weights]
    return pl.pallas_call(
        functools.partial(_merge_mlp_kernel, n_lat_tiles, batch, len(xs), has_ctx, final),
        grid=(batch, n_tiles),
        in_specs=specs,
        out_specs=rows(d),
        out_shape=jax.ShapeDtypeStruct((batch, n_tiles * tm, d), F32),
        compiler_params=_params("parallel", "parallel"),
        name="merge_and_mlp",
    )(*args)


def _arrange_w_in(w_in):
    splits = np.cumsum([F_WIDTH, 3 * DN_WIDTH, DN_WIDTH, 4 * DN_HEADS, DA_WIDTH, DA_WIDTH, DA_WIDTH])
    ab0, ab1 = int(splits[2]), int(splits[3])
    w_ab = w_in[..., ab0:ab1]
    h = DN_HEADS
    pad = jnp.zeros(w_in.shape[:-1] + (128 - 2 * h,), w_in.dtype)
    ab = jnp.concatenate([w_ab[..., 0:h], w_ab[..., 2 * h:3 * h], pad,
                          w_ab[..., h:2 * h], w_ab[..., 3 * h:4 * h], pad], axis=-1)
    return w_in[..., :ab0].astype(BF16), w_in[..., ab1:].astype(BF16), ab.astype(BF16)


def _rope_tables(n_lat, n_ctx):
    t = jnp.arange(n_lat)
    row = (t // GRID_W).astype(F32)
    col = (t % GRID_W).astype(F32)
    n_freq = DA_QK_DIM // 4
    inv_freq = ROPE_THETA ** (-jnp.arange(n_freq, dtype=F32) / n_freq)
    ang_r = row[:, None] * inv_freq[None, :]
    ang_c = col[:, None] * inv_freq[None, :]
    ang = jnp.concatenate([ang_r, ang_r, ang_c, ang_c], axis=1)
    sign = jnp.asarray(np.where((np.arange(DA_QK_DIM) & 16) == 0, -1.0, 1.0), F32)
    cos = jnp.concatenate([jnp.tile(jnp.cos(ang), (1, 2)), jnp.ones((n_ctx, 128), F32)], axis=0)
    sin = jnp.concatenate([jnp.tile(jnp.sin(ang) * sign, (1, 2)), jnp.zeros((n_ctx, 128), F32)], axis=0)
    return cos, sin


def _decay_params(a_log, dt_bias):
    h = DN_HEADS
    out = jnp.zeros((8, 256), F32)
    for d in range(2):
        out = out.at[0, d * 128 + h:d * 128 + 2 * h].set(a_log[d])
        out = out.at[1, d * 128 + h:d * 128 + 2 * h].set(dt_bias[d])
    return out


def kernel(x, c, ctx, c_ctx, norm1, norm2, w_ada, b_ada, w_in, conv_w, a_log, dt_bias, dn_gain, lam_vecs,
           da_gain, w_f, w_dn, w_da, w_o, w_mlp1, w_mlp2, final_norm):
    batch, n_lat, d = x.shape
    n_ctx = ctx.shape[1]
    depth = w_in.shape[0]
    t_all = n_lat + n_ctx
    n_lat_tiles = n_lat // ROW_TILE
    n_tiles = t_all // ROW_TILE

    cvec = jnp.zeros((8, d), F32).at[:batch].set(c).at[batch].set(c_ctx)
    mods = ada_modulation(cvec, w_ada, b_ada)
    cos_t, sin_t = _rope_tables(n_lat, n_ctx)
    xs = (x, ctx)
    final_gain = final_norm.reshape(1, d)
    w_big_all = _arrange_w_in(w_in)
    w_f16, w_dn16, w_da16, w_o16, w_mlp1_16, w_mlp2_16 = (w.astype(BF16) for w in (w_f, w_dn, w_da, w_o, w_mlp1, w_mlp2))

    for l in range(depth):
        last = l == depth - 1
        lam_init = 0.8 - 0.6 * math.exp(-0.3 * l)
        mod = mods[l]
        w_big = [(part, l) for part in w_big_all]
        abp = _decay_params(a_log[l], dt_bias[l])
        uf, qkv, z, q, k, v, gates, gb = input_projection(
            xs, mod, norm1[l].reshape(1, d), w_big, cos_t, sin_t, abp, n_lat_tiles)

        yf = fourier_latent(uf, n_lat)
        conv_w8 = jnp.zeros((8, 3 * DN_WIDTH), F32).at[:DN_CONV].set(conv_w[l])
        qn, kn, vn = deltanet_prep(qkv, conv_w8, n_lat_tiles)
        odn = deltanet_scan(qn, kn, vn, gb, n_lat)
        gain_da = da_gain[l].reshape(1, DA_V_DIM)
        tk = next(t for t in (768, 512, 256) if t_all % t == 0)
        tq = min(ATTN_Q_TILE, n_lat)
        oda = diff_attention(q, k, v, lam_vecs[l], gain_da, lam_init, n_lat, 0, t_all, 0, tq, tk)
        if last:
            yf_ctx = oda_ctx = None
            tiles = n_lat_tiles
        else:
            yf_ctx = fourier_context(uf, n_lat, n_ctx)
            oda_ctx = diff_attention(q, k, v, lam_vecs[l], gain_da, lam_init, n_ctx, n_lat, n_ctx, n_lat,
                                     n_ctx, n_ctx)
            tiles = n_tiles
        out = merge_and_mlp(xs, mod, yf, yf_ctx, odn, z, oda, oda_ctx, gates, dn_gain[l].reshape(1, DN_HEAD_DIM),
                            (w_f16, l), (w_dn16, l), (w_da16, l), (w_o16, l),
                            norm2[l].reshape(1, d), (w_mlp1_16, l), (w_mlp2_16, l), final_gain,
                            n_lat_tiles, tiles, last)
        xs = (out,)
    return xs[0]
```
